```python
import jax, jax.numpy as jnp
from jax import lax
import numpy as np

D_MODEL = 1024
BATCH = 8
SEQ = 8192
DEPTH = 2

GLA_HEADS = 4
GLA_DK = 32
GLA_DV = 64
GLA_RANK = 16
GLA_TAU = 16.0
GLA_CHUNK = 64
SWA_HEADS = 8
SWA_KV_HEADS = 2
SWA_HD = 64
WINDOW = 128
BLOCK = 128
N_BUCKETS = 32
MAX_DISTANCE = 128
CONV_CH = 256
CONV_WIDTH = 31
D_FF = 2816
FFN_CONV_WIDTH = 3
EPS = 1e-6

GLA_QK = GLA_HEADS * GLA_DK
GLA_V = GLA_HEADS * GLA_DV
SWA_Q = SWA_HEADS * SWA_HD
SWA_KV = SWA_KV_HEADS * SWA_HD
D_MIX = GLA_V + SWA_Q + CONV_CH
IN_SPLITS = (GLA_QK, GLA_QK, GLA_V, GLA_RANK, GLA_V, SWA_Q, SWA_KV, SWA_KV, CONV_CH, CONV_CH)
D_IN = GLA_QK + GLA_QK + GLA_V + GLA_RANK + GLA_V + SWA_Q + SWA_KV + SWA_KV + CONV_CH + CONV_CH

kernel_name = "hymba_gla_swa_conformer_convffn"


def _rms(x, g):
    x32 = x.astype(jnp.float32)
    y = x32 * lax.rsqrt(jnp.mean(x32 * x32, axis=-1, keepdims=True) + EPS)
    return (y * g.astype(jnp.float32)).astype(x.dtype)


def _causal_dwconv(x, w, b):
    width, ch = w.shape
    y = lax.conv_general_dilated(x, w[:, None, :].astype(x.dtype), window_strides=(1,),
                                 padding=[(width - 1, 0)],
                                 dimension_numbers=('NWC', 'WIO', 'NWC'),
                                 feature_group_count=ch)
    return y + b.astype(x.dtype)


def _t5_bucket(dist):
    max_exact = N_BUCKETS // 2
    d = jnp.maximum(dist, 0)
    d_f = jnp.maximum(d, 1).astype(jnp.float32)
    large = max_exact + (jnp.log(d_f / max_exact) / np.float32(np.log(MAX_DISTANCE / max_exact))
                         * (N_BUCKETS - max_exact)).astype(jnp.int32)
    large = jnp.minimum(large, N_BUCKETS - 1)
    return jnp.where(d < max_exact, d, large)


def _band_bias_and_mask(rel_bias):
    i = jnp.arange(BLOCK)[:, None]
    j = jnp.arange(2 * BLOCK)[None, :]
    dist = BLOCK + i - j
    band = (dist >= 0) & (dist < WINDOW)
    bias = rel_bias.astype(jnp.float32)[_t5_bucket(dist)]
    return jnp.transpose(bias, (2, 0, 1)), band


def _gla(q, k, v, z, r, w_a2, b_a, out_g):
    B, S = q.shape[:2]
    H, C = GLA_HEADS, GLA_CHUNK
    N = S // C
    f32 = jnp.float32
    logit = (z @ w_a2 + b_a).astype(f32)
    log_a = jax.nn.log_sigmoid(logit) / GLA_TAU

    def chunks(t, d):
        return t.astype(f32).reshape(B, N, C, H, d).transpose(0, 3, 1, 2, 4)

    qc = chunks(q, GLA_DK) * (GLA_DK ** -0.5)
    kc = chunks(k, GLA_DK)
    vc = chunks(v, GLA_DV)
    bcum = jnp.cumsum(chunks(log_a, GLA_DK), axis=3)
    b_last = bcum[:, :, :, -1:, :]
    q_dec = qc * jnp.exp(bcum)
    k_inv = kc * jnp.exp(-bcum)
    k_tail = kc * jnp.exp(b_last - bcum)
    causal = jnp.tril(jnp.ones((C, C), dtype=bool))
    attn = jnp.where(causal, jnp.einsum('bhnid,bhnjd->bhnij', q_dec, k_inv), 0.0)
    o_intra = jnp.einsum('bhnij,bhnjv->bhniv', attn, vc)
    dS = jnp.einsum('bhnjd,bhnjv->bhndv', k_tail, vc)
    decay = jnp.exp(b_last[:, :, :, 0, :])

    def step(s_prev, inp):
        dec, ds = inp
        return dec[..., None] * s_prev + ds, s_prev

    s0 = jnp.zeros((B, H, GLA_DK, GLA_DV), f32)
    _, s_before = lax.scan(step, s0, (jnp.moveaxis(decay, 2, 0), jnp.moveaxis(dS, 2, 0)))
    s_before = jnp.moveaxis(s_before, 0, 2)
    o = o_intra + jnp.einsum('bhnid,bhndv->bhniv', q_dec, s_before)
    o = o.transpose(0, 2, 3, 1, 4).reshape(B, S, H, GLA_DV)
    o = o * lax.rsqrt(jnp.mean(o * o, axis=-1, keepdims=True) + EPS)
    o = o.reshape(B, S, GLA_V) * out_g.astype(f32)
    o = o * jax.nn.silu(r.astype(f32))
    return o.astype(q.dtype)


def _swa(q, k, v, q_g, k_g, sinks, band_bias, band):
    B, S = q.shape[:2]
    nb = S // BLOCK
    G = SWA_HEADS // SWA_KV_HEADS
    q = _rms(q.reshape(B, S, SWA_HEADS, SWA_HD), q_g)
    k = _rms(k.reshape(B, S, SWA_KV_HEADS, SWA_HD), k_g)
    v = v.reshape(B, S, SWA_KV_HEADS, SWA_HD)
    qb = q.reshape(B, nb, BLOCK, SWA_KV_HEADS, G, SWA_HD)
    pad = ((0, 0), (BLOCK, 0), (0, 0), (0, 0))
    kb = jnp.pad(k, pad).reshape(B, nb + 1, BLOCK, SWA_KV_HEADS, SWA_HD)
    vb = jnp.pad(v, pad).reshape(B, nb + 1, BLOCK, SWA_KV_HEADS, SWA_HD)
    kwin = jnp.concatenate([kb[:, :-1], kb[:, 1:]], axis=2)
    vwin = jnp.concatenate([vb[:, :-1], vb[:, 1:]], axis=2)
    scores = jnp.einsum('bnqhgd,bnkhd->bnhgqk', qb, kwin,
                        preferred_element_type=jnp.float32) * (SWA_HD ** -0.5)
    scores = scores + band_bias.reshape(SWA_KV_HEADS, G, BLOCK, 2 * BLOCK)
    key_pos = (jnp.arange(nb)[:, None, None] * BLOCK - BLOCK + jnp.arange(2 * BLOCK)[None, None, :])
    mask = band[None] & (key_pos >= 0)
    scores = jnp.where(mask[None, :, None, None], scores, -1e30)
    sink = sinks.astype(jnp.float32).reshape(1, 1, SWA_KV_HEADS, G, 1, 1)
    m = jnp.maximum(jnp.max(scores, axis=-1, keepdims=True), sink)
    p = jnp.exp(scores - m)
    probs = p / (jnp.sum(p, axis=-1, keepdims=True) + jnp.exp(sink - m))
    out = jnp.einsum('bnhgqk,bnkhd->bnqhgd', probs.astype(v.dtype), vwin)
    return out.reshape(B, S, SWA_Q)


def _conformer_conv(a, gate, dw_w, dw_b, ln_g, ln_b):
    u = a * jax.nn.sigmoid(gate)
    y = _causal_dwconv(u, dw_w, dw_b).astype(jnp.float32)
    mu = jnp.mean(y, axis=-1, keepdims=True)
    var = jnp.mean(jnp.square(y - mu), axis=-1, keepdims=True)
    y = (y - mu) * lax.rsqrt(var + EPS) * ln_g.astype(jnp.float32) + ln_b.astype(jnp.float32)
    return jax.nn.silu(y).astype(a.dtype)


def setup_inputs(seed: int = 0) -> dict:
    key = jax.random.key(seed)
    ks = jax.random.split(key, 24)
    f32 = jnp.float32

    def nrm(k, shape, scale):
        return jax.random.normal(k, shape, f32) * scale

    L = DEPTH
    return {
        "x": nrm(ks[0], (BATCH, SEQ, D_MODEL), 1.0),
        "attn_norm_g": 1.0 + nrm(ks[1], (L, D_MODEL), 0.02),
        "w_in": nrm(ks[2], (L, D_MODEL, D_IN), D_MODEL ** -0.5),
        "gla_w_a2": nrm(ks[3], (L, GLA_RANK, GLA_QK), GLA_RANK ** -0.5),
        "gla_b_a": nrm(ks[4], (L, GLA_QK), 0.1),
        "gla_out_g": 1.0 + nrm(ks[5], (L, GLA_V), 0.02),
        "swa_q_g": 1.0 + nrm(ks[6], (L, SWA_HD), 0.02),
        "swa_k_g": 1.0 + nrm(ks[7], (L, SWA_HD), 0.02),
        "swa_sinks": nrm(ks[8], (L, SWA_HEADS), 0.5),
        "rel_bias": nrm(ks[9], (N_BUCKETS, SWA_HEADS), 0.5),
        "conv_dw_w": nrm(ks[10], (L, CONV_WIDTH, CONV_CH), CONV_WIDTH ** -0.5),
        "conv_dw_b": nrm(ks[11], (L, CONV_CH), 0.02),
        "conv_ln_g": 1.0 + nrm(ks[12], (L, CONV_CH), 0.02),
        "conv_ln_b": nrm(ks[13], (L, CONV_CH), 0.02),
        "branch_scale": 1.0 + nrm(ks[14], (L, D_MIX), 0.02),
        "w_out": nrm(ks[15], (L, D_MIX, D_MODEL), (2.0 * DEPTH * D_MIX) ** -0.5),
        "ffn_norm_g": 1.0 + nrm(ks[16], (L, D_MODEL), 0.02),
        "w_up": nrm(ks[17], (L, D_MODEL, 2 * D_FF), D_MODEL ** -0.5),
        "ffn_conv_w": nrm(ks[18], (L, FFN_CONV_WIDTH, 2 * D_FF), FFN_CONV_WIDTH ** -0.5),
        "ffn_conv_b": nrm(ks[19], (L, 2 * D_FF), 0.02),
        "w_down": nrm(ks[20], (L, D_FF, D_MODEL), (2.0 * DEPTH * D_FF) ** -0.5),
    }


def reference(x, attn_norm_g, w_in, gla_w_a2, gla_b_a, gla_out_g, swa_q_g, swa_k_g, swa_sinks,
              rel_bias, conv_dw_w, conv_dw_b, conv_ln_g, conv_ln_b, branch_scale, w_out,
              ffn_norm_g, w_up, ffn_conv_w, ffn_conv_b, w_down):
    band_bias, band = _band_bias_and_mask(rel_bias)
    split_idx = list(np.cumsum(IN_SPLITS)[:-1])
    for l in range(DEPTH):
        h = _rms(x, attn_norm_g[l])
        proj = h @ w_in[l]
        (q_a, k_a, v_a, z_a, r_a, q_b, k_b, v_b, c_a, c_gate) = jnp.split(proj, split_idx, axis=-1)
        o_a = _gla(q_a, k_a, v_a, z_a, r_a, gla_w_a2[l], gla_b_a[l], gla_out_g[l])
        o_b = _swa(q_b, k_b, v_b, swa_q_g[l], swa_k_g[l], swa_sinks[l], band_bias, band)
        o_c = _conformer_conv(c_a, c_gate, conv_dw_w[l], conv_dw_b[l], conv_ln_g[l], conv_ln_b[l])
        mix = jnp.concatenate([o_a, o_b, o_c], axis=-1) * branch_scale[l]
        x = x + mix @ w_out[l]
        h = _rms(x, ffn_norm_g[l])
        u = _causal_dwconv(h @ w_up[l], ffn_conv_w[l], ffn_conv_b[l])
        gate, val = jnp.split(u, 2, axis=-1)
        x = x + (jax.nn.silu(gate) * val) @ w_down[l]
    return x
```

```python
import functools

import numpy as np
import jax
import jax.numpy as jnp
from jax import lax
from jax.experimental import pallas as pl
from jax.experimental.pallas import tpu as pltpu

_F32 = jnp.float32
_BF16 = jnp.bfloat16

D_MODEL = 1024
DEPTH = 2
GLA_HEADS = 4
GLA_DK = 32
GLA_DV = 64
GLA_RANK = 16
GLA_TAU = 16.0
GLA_CHUNK = 64
SWA_HEADS = 8
SWA_KV_HEADS = 2
SWA_HD = 64
WINDOW = 128
BLOCK = 128
N_BUCKETS = 32
MAX_DISTANCE = 128
CONV_CH = 256
CONV_WIDTH = 31
D_FF = 2816
FFN_CONV_WIDTH = 3
EPS = 1e-6
NEG_INF = -1e30

GLA_QK = GLA_HEADS * GLA_DK
GLA_V = GLA_HEADS * GLA_DV
SWA_Q = SWA_HEADS * SWA_HD
SWA_KV = SWA_KV_HEADS * SWA_HD
D_MIX = GLA_V + SWA_Q + CONV_CH
SWA_PAIRS = SWA_HEADS // SWA_KV_HEADS // 2

LANES = 128
SUBLANES = 8
VMEM_LIMIT_BYTES = 56 * 1024 * 1024

TILE = 512
CONV_TAIL = 32
FFN_TAIL = SUBLANES
FFN_CHUNK = 256
N_FFN_CHUNKS = D_FF // FFN_CHUNK

OFF_QA = 0
OFF_KA = OFF_QA + GLA_QK
OFF_Z = OFF_KA + GLA_QK
OFF_VA = OFF_Z + LANES
OFF_RA = OFF_VA + GLA_V
OFF_QB = OFF_RA + GLA_V
OFF_KB = OFF_QB + SWA_Q
OFF_VB = OFF_KB + SWA_KV
OFF_CA = OFF_VB + SWA_KV
OFF_CG = OFF_CA + CONV_CH
D_PROJ = OFF_CG + CONV_CH
PROJ_GROUPS = ((OFF_QA, OFF_VA), (OFF_VA, OFF_QB), (OFF_QB, OFF_KB), (OFF_KB, OFF_CA), (OFF_CA, D_PROJ))


def _dot(a, b):
    return jnp.dot(a, b, preferred_element_type=_F32)


def _dot_nt(a, b):
    return lax.dot_general(a, b, (((1,), (1,)), ((), ())), preferred_element_type=_F32)


def _split_dot(x, w):
    hi = x.astype(_BF16)
    lo = (x - hi.astype(_F32)).astype(_BF16)
    return _dot(hi, w) + _dot(lo, w)


def _rms_rows(x, g):
    ms = jnp.mean(x * x, axis=-1, keepdims=True)
    return x * lax.rsqrt(ms + EPS) * g


def _bias_kernel(rb_ref, bucket_ref, out_ref):
    h = pl.program_id(0)
    bk = bucket_ref[...]
    acc = jnp.full(bk.shape, NEG_INF, _F32)
    for b in range(N_BUCKETS):
        acc = jnp.where(bk == b, rb_ref[b, h], acc)
    out_ref[0] = acc


def _band_buckets():
    i = jnp.arange(BLOCK)[:, None]
    j = jnp.arange(2 * BLOCK)[None, :]
    dist = BLOCK + i - j
    band = (dist >= 0) & (dist < WINDOW)
    max_exact = N_BUCKETS // 2
    d = jnp.maximum(dist, 0)
    d_f = jnp.maximum(d, 1).astype(_F32)
    large = max_exact + (jnp.log(d_f / max_exact) / np.float32(np.log(MAX_DISTANCE / max_exact))
                         * (N_BUCKETS - max_exact)).astype(jnp.int32)
    large = jnp.minimum(large, N_BUCKETS - 1)
    bucket = jnp.where(d < max_exact, d, large)
    return jnp.where(band, bucket, -1).astype(jnp.int32)


def _band_bias(rel_bias):
    return pl.pallas_call(
        _bias_kernel,
        grid=(SWA_HEADS,),
        in_specs=[
            pl.BlockSpec(memory_space=pltpu.SMEM),
            pl.BlockSpec((BLOCK, 2 * BLOCK), lambda h: (0, 0)),
        ],
        out_specs=pl.BlockSpec((1, BLOCK, 2 * BLOCK), lambda h: (h, 0, 0)),
        out_shape=jax.ShapeDtypeStruct((SWA_HEADS, BLOCK, 2 * BLOCK), _F32),
        name="band_bias",
    )(rel_bias.astype(_F32), _band_buckets())


def _gla(proj, wa2_ref, ba_ref, tri_ref, st_ref, oa_ref):
    row = lax.broadcasted_iota(jnp.int32, (GLA_HEADS * GLA_CHUNK, GLA_CHUNK), 0)
    col = lax.broadcasted_iota(jnp.int32, (GLA_HEADS * GLA_CHUNK, GLA_CHUNK), 1)
    causal = col <= (row % GLA_CHUNK)
    qlane = lax.broadcasted_iota(jnp.int32, (GLA_CHUNK, GLA_QK), 1) // GLA_DK
    vlane = lax.broadcasted_iota(jnp.int32, (GLA_CHUNK, GLA_V), 1) // GLA_DV
    st_row = lax.broadcasted_iota(jnp.int32, (GLA_V, GLA_QK), 0) // GLA_DV
    st_col = lax.broadcasted_iota(jnp.int32, (GLA_V, GLA_QK), 1) // GLA_DK
    same_head = st_row == st_col
    tri = tri_ref[...]
    for c in range(TILE // GLA_CHUNK):
        rows = pl.ds(c * GLA_CHUNK, GLA_CHUNK)
        q = proj[rows, OFF_QA:OFF_QA + GLA_QK] * (GLA_DK ** -0.5)
        k = proj[rows, OFF_KA:OFF_KA + GLA_QK]
        z = proj[rows, OFF_Z:OFF_Z + LANES]
        v = proj[rows, OFF_VA:OFF_VA + GLA_V]
        logit = _dot(z.astype(_BF16), wa2_ref[...]) + ba_ref[...]
        log_a = jax.nn.log_sigmoid(logit) / GLA_TAU
        bcum = _split_dot_lhs(tri, log_a)
        b_last = bcum[GLA_CHUNK - 1:GLA_CHUNK, :]
        q_dec = (q * jnp.exp(bcum)).astype(_BF16)
        k_inv = (k * jnp.exp(-bcum)).astype(_BF16)
        k_tail = (k * jnp.exp(b_last - bcum)).astype(_BF16)
        decay = jnp.exp(b_last)
        v16 = v.astype(_BF16)
        q_heads = jnp.concatenate(
            [jnp.where(qlane == h, q_dec, jnp.zeros_like(q_dec)) for h in range(GLA_HEADS)], axis=0)
        attn = jnp.where(causal, _dot_nt(q_heads, k_inv), 0.0).astype(_BF16)
        full = _dot(attn, v16)
        o = _dot_nt(q_dec, st_ref[...].astype(_BF16))
        for h in range(GLA_HEADS):
            o = o + jnp.where(vlane == h, full[h * GLA_CHUNK:(h + 1) * GLA_CHUNK, :], 0.0)
        oa_ref[rows, :] = o
        d_st = _dot(v.T.astype(_BF16), k_tail)
        st_ref[...] = st_ref[...] * decay + jnp.where(same_head, d_st, 0.0)


def _split_dot_lhs(w, x):
    hi = x.astype(_BF16)
    lo = (x - hi.astype(_F32)).astype(_BF16)
    return _dot(w, hi) + _dot(w, lo)


def _swa(first, proj, sinks_ref, qg_ref, kg_ref, bias_ref, seg_ref, bs_ref, qs_ref, kv_ref, mix_ref):
    seg = seg_ref[...]
    lane = lax.broadcasted_iota(jnp.int32, (TILE, LANES), 1)
    low = lane < SWA_HD
    q_gain = qg_ref[...] * (SWA_HD ** -0.5)
    for j in range(SWA_Q // LANES):
        q = proj[:, OFF_QB + j * LANES:OFF_QB + (j + 1) * LANES]
        ms = _split_dot(q * q, seg) * (1.0 / SWA_HD)
        qs_ref[:, j * LANES:(j + 1) * LANES] = (
            q * lax.rsqrt(ms + EPS) * q_gain[:, j * LANES:(j + 1) * LANES]).astype(_BF16)
    k = proj[:, OFF_KB:OFF_KB + SWA_KV]
    ms = _split_dot(k * k, seg) * (1.0 / SWA_HD)
    k = k * lax.rsqrt(ms + EPS) * kg_ref[...]
    k_sw = pltpu.roll(k, SWA_HD, axis=1)
    v = proj[:, OFF_VB:OFF_VB + SWA_KV]
    v_sw = pltpu.roll(v, SWA_HD, axis=1)
    cur = pl.ds(BLOCK, TILE)
    zero = jnp.zeros_like(k)
    kv_ref[0, cur, :] = jnp.where(low, k, zero).astype(_BF16)
    kv_ref[1, cur, :] = jnp.where(low, zero, k_sw).astype(_BF16)
    kv_ref[2, cur, :] = jnp.where(low, k_sw, zero).astype(_BF16)
    kv_ref[3, cur, :] = jnp.where(low, zero, k).astype(_BF16)
    kv_ref[4, cur, :] = v.astype(_BF16)
    kv_ref[5, cur, :] = v_sw.astype(_BF16)

    rows2 = lax.broadcasted_iota(jnp.int32, (SWA_PAIRS * BLOCK, 1), 0)
    prev_key = lax.broadcasted_iota(jnp.int32, (SWA_PAIRS * BLOCK, 2 * BLOCK), 1) < BLOCK
    low_b = lax.broadcasted_iota(jnp.int32, (BLOCK, LANES), 1) < SWA_HD
    for n in range(TILE // BLOCK):
        win = pl.ds(n * BLOCK, 2 * BLOCK)
        qrows = pl.ds(n * BLOCK, BLOCK)
        for kvh in range(SWA_KV_HEADS):
            q2 = jnp.concatenate(
                [qs_ref[qrows, (SWA_PAIRS * kvh + i) * LANES:(SWA_PAIRS * kvh + i + 1) * LANES]
                 for i in range(SWA_PAIRS)], axis=0)
            outs = []
            for par in range(2):
                s = _dot_nt(q2, kv_ref[2 * kvh + par, win, :]) + bias_ref[2 * kvh + par]
                if n == 0:
                    s = jnp.where(jnp.logical_and(first, prev_key), NEG_INF, s)
                sink = jnp.full((SWA_PAIRS * BLOCK, 1), sinks_ref[2 * SWA_PAIRS * kvh + par], _F32)
                for i in range(1, SWA_PAIRS):
                    sink = jnp.where(rows2 >= i * BLOCK,
                                     sinks_ref[2 * (SWA_PAIRS * kvh + i) + par], sink)
                m = jnp.maximum(jnp.max(s, axis=-1, keepdims=True), sink)
                p = jnp.exp(s - m)
                denom = jnp.sum(p, axis=-1, keepdims=True) + jnp.exp(sink - m)
                v_idx = 4 + ((kvh + par) % 2)
                outs.append(_dot(p.astype(_BF16), kv_ref[v_idx, win, :]) / denom)
            for i in range(SWA_PAIRS):
                blk = slice(i * BLOCK, (i + 1) * BLOCK)
                o = jnp.where(low_b, outs[0][blk], outs[1][blk])
                c0 = GLA_V + (SWA_PAIRS * kvh + i) * LANES
                mix_ref[qrows, c0:c0 + LANES] = (o * bs_ref[:, c0:c0 + LANES]).astype(_BF16)
    for a in range(6):
        kv_ref[a, 0:BLOCK, :] = kv_ref[a, TILE:TILE + BLOCK, :]


def _conformer(proj, cw_ref, cb_ref, lng_ref, lnb_ref, bs_ref, cu_ref, mix_ref):
    ca = proj[:, OFF_CA:OFF_CA + CONV_CH]
    cg = proj[:, OFF_CG:OFF_CG + CONV_CH]
    cu_ref[CONV_TAIL:CONV_TAIL + TILE, :] = ca * jax.nn.sigmoid(cg)
    rb = 64
    base = CONV_TAIL - (CONV_WIDTH - 1)
    c0 = GLA_V + SWA_Q
    for r in range(TILE // rb):
        acc = jnp.zeros((rb, CONV_CH), _F32)
        for t in range(CONV_WIDTH):
            acc = acc + cw_ref[t:t + 1, :] * cu_ref[r * rb + base + t:r * rb + base + t + rb, :]
        y = acc + cb_ref[...]
        mu = jnp.mean(y, axis=-1, keepdims=True)
        var = jnp.mean(jnp.square(y - mu), axis=-1, keepdims=True)
        y = (y - mu) * lax.rsqrt(var + EPS) * lng_ref[...] + lnb_ref[...]
        y = jax.nn.silu(y)
        mix_ref[r * rb:(r + 1) * rb, c0:c0 + CONV_CH] = (y * bs_ref[:, c0:c0 + CONV_CH]).astype(_BF16)
    cu_ref[0:CONV_TAIL, :] = cu_ref[TILE:TILE + CONV_TAIL, :]


def _mixer_kernel(sinks_ref, x_ref, g_ref, win_ref, wa2_ref, ba_ref, outg_ref, qg_ref, kg_ref,
                  bias_ref, cw_ref, cb_ref, lng_ref, lnb_ref, bs_ref, wout_ref, tri_ref,
                  seg_v_ref, seg_ref, o_ref,
                  h_ref, proj, st_ref, oa_ref, qs_ref, kv_ref, cu_ref, mix_ref):
    first = pl.program_id(1) == 0

    @pl.when(first)
    def _():
        st_ref[...] = jnp.zeros_like(st_ref)
        kv_ref[:, 0:BLOCK, :] = jnp.zeros((6, BLOCK, LANES), _BF16)
        cu_ref[0:CONV_TAIL, :] = jnp.zeros((CONV_TAIL, CONV_CH), _F32)

    h_ref[...] = _rms_rows(x_ref[0], g_ref[...]).astype(_BF16)
    for lo, hi in PROJ_GROUPS:
        proj[:, lo:hi] = _dot(h_ref[...], win_ref[:, lo:hi])

    _gla(proj, wa2_ref, ba_ref, tri_ref, st_ref, oa_ref)
    o = oa_ref[...]
    ms = _split_dot(o * o, seg_v_ref[...]) * (1.0 / GLA_DV)
    o = o * lax.rsqrt(ms + EPS)
    o = o * outg_ref[...]
    o = o * jax.nn.silu(proj[:, OFF_RA:OFF_RA + GLA_V])
    mix_ref[:, 0:GLA_V] = (o * bs_ref[:, 0:GLA_V]).astype(_BF16)

    _swa(first, proj, sinks_ref, qg_ref, kg_ref, bias_ref, seg_ref, bs_ref, qs_ref, kv_ref, mix_ref)

    _conformer(proj, cw_ref, cb_ref, lng_ref, lnb_ref, bs_ref, cu_ref, mix_ref)

    o_ref[0] = x_ref[0] + _dot(mix_ref[...], wout_ref[...])


def _const_spec(shape):
    nd = len(shape)
    return pl.BlockSpec(shape, lambda b, s: (0,) * nd, pipeline_mode=pl.Buffered(1))


def _seg_ones(n, width):
    idx = np.arange(n) // width
    return jnp.asarray((idx[:, None] == idx[None, :]).astype(np.float32), dtype=_BF16)


def _mixer(x, sinks, g, w_in_r, wa2, ba, outg, qg, kg, bias, cw, cb, lng, lnb, bs, w_out):
    B, S, _ = x.shape
    tri = jnp.asarray(np.tril(np.ones((GLA_CHUNK, GLA_CHUNK), np.float32)), dtype=_BF16)
    seg_v = _seg_ones(GLA_V, GLA_DV)
    seg = _seg_ones(LANES, SWA_HD)
    consts = (g, w_in_r, wa2, ba, outg, qg, kg, bias, cw, cb, lng, lnb, bs, w_out, tri, seg_v, seg)
    return pl.pallas_call(
        _mixer_kernel,
        grid=(B, S // TILE),
        in_specs=[pl.BlockSpec(memory_space=pltpu.SMEM),
                  pl.BlockSpec((1, TILE, D_MODEL), lambda b, s: (b, s, 0))]
        + [_const_spec(c.shape) for c in consts],
        out_specs=pl.BlockSpec((1, TILE, D_MODEL), lambda b, s: (b, s, 0)),
        scratch_shapes=[
            pltpu.VMEM((TILE, D_MODEL), _BF16),
            pltpu.VMEM((TILE, D_PROJ), _F32),
            pltpu.VMEM((GLA_V, GLA_QK), _F32),
            pltpu.VMEM((TILE, GLA_V), _F32),
            pltpu.VMEM((TILE, SWA_Q), _BF16),
            pltpu.VMEM((6, BLOCK + TILE, LANES), _BF16),
            pltpu.VMEM((CONV_TAIL + TILE, CONV_CH), _F32),
            pltpu.VMEM((TILE, D_MIX), _BF16),
        ],
        out_shape=jax.ShapeDtypeStruct(x.shape, _F32),
        compiler_params=pltpu.CompilerParams(
            dimension_semantics=("arbitrary", "arbitrary"),
            vmem_limit_bytes=VMEM_LIMIT_BYTES),
        name="mixer",
    )(sinks, x, *consts)


def _ffn_kernel(x_ref, g_ref, wup_ref, cw_ref, cb_ref, wdown_ref, o_ref,
                h_ref, u_ref, tail_ref, act_ref, acc_ref):
    @pl.when(pl.program_id(1) == 0)
    def _():
        tail_ref[...] = jnp.zeros_like(tail_ref)

    h_ref[...] = _rms_rows(x_ref[0], g_ref[...]).astype(_BF16)
    acc_ref[...] = jnp.zeros_like(acc_ref)
    rb = 64

    def chunk(c, carry):
        for half in range(2):
            idx = c + half * N_FFN_CHUNKS
            cols = slice(half * FFN_CHUNK, (half + 1) * FFN_CHUNK)
            u_ref[0:FFN_TAIL, cols] = tail_ref[idx]
            u_ref[FFN_TAIL:FFN_TAIL + TILE, cols] = _dot(h_ref[...], wup_ref[idx])
            tail_ref[idx] = u_ref[TILE:TILE + FFN_TAIL, cols]
        w = [jnp.concatenate([cw_ref[c, t:t + 1, :], cw_ref[c + N_FFN_CHUNKS, t:t + 1, :]], axis=-1)
             for t in range(FFN_CONV_WIDTH)]
        b = jnp.concatenate([cb_ref[c], cb_ref[c + N_FFN_CHUNKS]], axis=-1)
        base = FFN_TAIL - (FFN_CONV_WIDTH - 1)
        for r in range(TILE // rb):
            y = b
            for t in range(FFN_CONV_WIDTH):
                y = y + w[t] * u_ref[r * rb + base + t:r * rb + base + t + rb, :]
            act = jax.nn.silu(y[:, :FFN_CHUNK]) * y[:, FFN_CHUNK:]
            act_ref[r * rb:(r + 1) * rb, :] = act.astype(_BF16)
        acc_ref[...] += _dot(act_ref[...], wdown_ref[c])
        return carry

    lax.fori_loop(0, N_FFN_CHUNKS, chunk, 0)
    o_ref[0] = x_ref[0] + acc_ref[...]


def _ffn(x, g, w_up_r, cw_r, cb_r, w_down_r):
    B, S, _ = x.shape
    consts = (g, w_up_r, cw_r, cb_r, w_down_r)
    return pl.pallas_call(
        _ffn_kernel,
        grid=(B, S // TILE),
        in_specs=[pl.BlockSpec((1, TILE, D_MODEL), lambda b, s: (b, s, 0))]
        + [_const_spec(c.shape) for c in consts],
        out_specs=pl.BlockSpec((1, TILE, D_MODEL), lambda b, s: (b, s, 0)),
        out_shape=jax.ShapeDtypeStruct(x.shape, _F32),
        scratch_shapes=[
            pltpu.VMEM((TILE, D_MODEL), _BF16),
            pltpu.VMEM((FFN_TAIL + TILE, 2 * FFN_CHUNK), _F32),
            pltpu.VMEM((2 * N_FFN_CHUNKS, FFN_TAIL, FFN_CHUNK), _F32),
            pltpu.VMEM((TILE, FFN_CHUNK), _BF16),
            pltpu.VMEM((TILE, D_MODEL), _F32),
        ],
        compiler_params=pltpu.CompilerParams(
            dimension_semantics=("arbitrary", "arbitrary"),
            vmem_limit_bytes=VMEM_LIMIT_BYTES),
        name="ffn",
    )(x, *consts)


def _chunk_cols(a):
    lead = a.shape[:-1]
    a = a.reshape(lead + (2 * N_FFN_CHUNKS, FFN_CHUNK))
    return jnp.moveaxis(a, -2, 0)


def kernel(x, attn_norm_g, w_in, gla_w_a2, gla_b_a, gla_out_g, swa_q_g, swa_k_g, swa_sinks, rel_bias, conv_dw_w, conv_dw_b, conv_ln_g, conv_ln_b, branch_scale, w_out, ffn_norm_g, w_up, ffn_conv_w, ffn_conv_b, w_down):
    bias = _band_bias(rel_bias).reshape(SWA_KV_HEADS, SWA_PAIRS, 2, BLOCK, 2 * BLOCK)
    bias = bias.transpose(0, 2, 1, 3, 4).reshape(2 * SWA_KV_HEADS, SWA_PAIRS * BLOCK, 2 * BLOCK)
    z0 = 2 * GLA_QK + GLA_V
    for l in range(DEPTH):
        w = w_in[l]
        w_in_r = jnp.concatenate(
            [w[:, :2 * GLA_QK],
             jnp.pad(w[:, z0:z0 + GLA_RANK], ((0, 0), (0, LANES - GLA_RANK))),
             w[:, 2 * GLA_QK:z0],
             w[:, z0 + GLA_RANK:]], axis=1).astype(_BF16)
        wa2 = jnp.pad(gla_w_a2[l], ((0, LANES - GLA_RANK), (0, 0))).astype(_BF16)
        x = _mixer(
            x, swa_sinks[l].astype(_F32),
            attn_norm_g[l].reshape(1, D_MODEL), w_in_r, wa2,
            gla_b_a[l].reshape(1, GLA_QK), gla_out_g[l].reshape(1, GLA_V),
            jnp.tile(swa_q_g[l], SWA_HEADS).reshape(1, SWA_Q),
            jnp.tile(swa_k_g[l], SWA_KV_HEADS).reshape(1, SWA_KV),
            bias,
            jnp.pad(conv_dw_w[l], ((0, CONV_TAIL - CONV_WIDTH), (0, 0))),
            conv_dw_b[l].reshape(1, CONV_CH), conv_ln_g[l].reshape(1, CONV_CH),
            conv_ln_b[l].reshape(1, CONV_CH), branch_scale[l].reshape(1, D_MIX),
            w_out[l].astype(_BF16))
        x = _ffn(
            x, ffn_norm_g[l].reshape(1, D_MODEL),
            _chunk_cols(w_up[l].astype(_BF16)),
            _chunk_cols(jnp.pad(ffn_conv_w[l], ((0, SUBLANES - FFN_CONV_WIDTH), (0, 0)))),
            _chunk_cols(ffn_conv_b[l].reshape(1, 2 * D_FF)),
            w_down[l].astype(_BF16).reshape(N_FFN_CHUNKS, FFN_CHUNK, D_MODEL))
    return x
```

```python
import functools

import numpy as np
import jax
import jax.numpy as jnp
from jax import lax
from jax.experimental import pallas as pl
from jax.experimental.pallas import tpu as pltpu

_F32 = jnp.float32
_BF16 = jnp.bfloat16

D_MODEL = 1024
DEPTH = 2
GLA_HEADS = 4
GLA_DK = 32
GLA_DV = 64
GLA_RANK = 16
GLA_TAU = 16.0
GLA_CHUNK = 64
SWA_HEADS = 8
SWA_KV_HEADS = 2
SWA_HD = 64
WINDOW = 128
BLOCK = 128
N_BUCKETS = 32
MAX_DISTANCE = 128
CONV_CH = 256
CONV_WIDTH = 31
D_FF = 2816
FFN_CONV_WIDTH = 3
EPS = 1e-6
NEG_INF = -1e30

GLA_QK = GLA_HEADS * GLA_DK
GLA_V = GLA_HEADS * GLA_DV
SWA_Q = SWA_HEADS * SWA_HD
SWA_KV = SWA_KV_HEADS * SWA_HD
D_MIX = GLA_V + SWA_Q + CONV_CH
SWA_PAIRS = SWA_HEADS // SWA_KV_HEADS // 2

LANES = 128
SUBLANES = 8
VMEM_LIMIT_BYTES = 56 * 1024 * 1024

TILE = 512
CONV_TAIL = 32
FFN_TAIL = SUBLANES
FFN_CHUNK = 256
N_FFN_CHUNKS = D_FF // FFN_CHUNK

OFF_QA = 0
OFF_KA = OFF_QA + GLA_QK
OFF_Z = OFF_KA + GLA_QK
OFF_VA = OFF_Z + LANES
OFF_RA = OFF_VA + GLA_V
OFF_QB = OFF_RA + GLA_V
OFF_KB = OFF_QB + SWA_Q
OFF_VB = OFF_KB + SWA_KV
OFF_CA = OFF_VB + SWA_KV
OFF_CG = OFF_CA + CONV_CH
D_PROJ = OFF_CG + CONV_CH
PROJ_GROUPS = ((OFF_QA, OFF_VA), (OFF_VA, OFF_QB), (OFF_QB, OFF_KB), (OFF_KB, OFF_CA), (OFF_CA, D_PROJ))


def _dot(a, b):
    return jnp.dot(a, b, preferred_element_type=_F32)


def _dot_nt(a, b):
    return lax.dot_general(a, b, (((1,), (1,)), ((), ())), preferred_element_type=_F32)


def _split_dot(x, w):
    hi = x.astype(_BF16)
    lo = (x - hi.astype(_F32)).astype(_BF16)
    return _dot(hi, w) + _dot(lo, w)


def _rms_rows(x, g):
    ms = jnp.mean(x * x, axis=-1, keepdims=True)
    return x * lax.rsqrt(ms + EPS) * g


def _bias_kernel(rb_ref, bucket_ref, out_ref):
    h = pl.program_id(0)
    bk = bucket_ref[...]
    acc = jnp.full(bk.shape, NEG_INF, _F32)
    for b in range(N_BUCKETS):
        acc = jnp.where(bk == b, rb_ref[b, h], acc)
    out_ref[0] = acc


def _band_buckets():
    i = jnp.arange(BLOCK)[:, None]
    j = jnp.arange(2 * BLOCK)[None, :]
    dist = BLOCK + i - j
    band = (dist >= 0) & (dist < WINDOW)
    max_exact = N_BUCKETS // 2
    d = jnp.maximum(dist, 0)
    d_f = jnp.maximum(d, 1).astype(_F32)
    large = max_exact + (jnp.log(d_f / max_exact) / np.float32(np.log(MAX_DISTANCE / max_exact))
                         * (N_BUCKETS - max_exact)).astype(jnp.int32)
    large = jnp.minimum(large, N_BUCKETS - 1)
    bucket = jnp.where(d < max_exact, d, large)
    return jnp.where(band, bucket, -1).astype(jnp.int32)


def _band_bias(rel_bias):
    return pl.pallas_call(
        _bias_kernel,
        grid=(SWA_HEADS,),
        in_specs=[
            pl.BlockSpec(memory_space=pltpu.SMEM),
            pl.BlockSpec((BLOCK, 2 * BLOCK), lambda h: (0, 0)),
        ],
        out_specs=pl.BlockSpec((1, BLOCK, 2 * BLOCK), lambda h: (h, 0, 0)),
        out_shape=jax.ShapeDtypeStruct((SWA_HEADS, BLOCK, 2 * BLOCK), _F32),
        name="band_bias",
    )(rel_bias.astype(_F32), _band_buckets())


def _gla(proj, wa2_ref, ba_ref, tri_ref, st_ref, oa_ref):
    row = lax.broadcasted_iota(jnp.int32, (GLA_HEADS * GLA_CHUNK, GLA_CHUNK), 0)
    col = lax.broadcasted_iota(jnp.int32, (GLA_HEADS * GLA_CHUNK, GLA_CHUNK), 1)
    causal = col <= (row % GLA_CHUNK)
    qlane = lax.broadcasted_iota(jnp.int32, (GLA_CHUNK, GLA_QK), 1) // GLA_DK
    vlane = lax.broadcasted_iota(jnp.int32, (GLA_CHUNK, GLA_V), 1) // GLA_DV
    st_row = lax.broadcasted_iota(jnp.int32, (GLA_V, GLA_QK), 0) // GLA_DV
    st_col = lax.broadcasted_iota(jnp.int32, (GLA_V, GLA_QK), 1) // GLA_DK
    same_head = st_row == st_col
    tri = tri_ref[...]
    for c in range(TILE // GLA_CHUNK):
        rows = pl.ds(c * GLA_CHUNK, GLA_CHUNK)
        q = proj[rows, OFF_QA:OFF_QA + GLA_QK] * (GLA_DK ** -0.5)
        k = proj[rows, OFF_KA:OFF_KA + GLA_QK]
        z = proj[rows, OFF_Z:OFF_Z + LANES]
        v = proj[rows, OFF_VA:OFF_VA + GLA_V]
        logit = _dot(z.astype(_BF16), wa2_ref[...]) + ba_ref[...]
        log_a = jax.nn.log_sigmoid(logit) / GLA_TAU
        bcum = _split_dot_lhs(tri, log_a)
        b_last = bcum[GLA_CHUNK - 1:GLA_CHUNK, :]
        q_dec = (q * jnp.exp(bcum)).astype(_BF16)
        k_inv = (k * jnp.exp(-bcum)).astype(_BF16)
        k_tail = (k * jnp.exp(b_last - bcum)).astype(_BF16)
        decay = jnp.exp(b_last)
        v16 = v.astype(_BF16)
        q_heads = jnp.concatenate(
            [jnp.where(qlane == h, q_dec, jnp.zeros_like(q_dec)) for h in range(GLA_HEADS)], axis=0)
        attn = jnp.where(causal, _dot_nt(q_heads, k_inv), 0.0).astype(_BF16)
        full = _dot(attn, v16)
        o = _dot_nt(q_dec, st_ref[...].astype(_BF16))
        for h in range(GLA_HEADS):
            o = o + jnp.where(vlane == h, full[h * GLA_CHUNK:(h + 1) * GLA_CHUNK, :], 0.0)
        oa_ref[rows, :] = o
        d_st = _dot(v.T.astype(_BF16), k_tail)
        st_ref[...] = st_ref[...] * decay + jnp.where(same_head, d_st, 0.0)


def _split_dot_lhs(w, x):
    hi = x.astype(_BF16)
    lo = (x - hi.astype(_F32)).astype(_BF16)
    return _dot(w, hi) + _dot(w, lo)


def _swa(first, proj, sinks_ref, qg_ref, kg_ref, bias_ref, seg_ref, bs_ref, qs_ref, kv_ref, mix_ref):
    seg = seg_ref[...]
    lane = lax.broadcasted_iota(jnp.int32, (TILE, LANES), 1)
    low = lane < SWA_HD
    q_gain = qg_ref[...] * (SWA_HD ** -0.5)
    for j in range(SWA_Q // LANES):
        q = proj[:, OFF_QB + j * LANES:OFF_QB + (j + 1) * LANES]
        ms = _split_dot(q * q, seg) * (1.0 / SWA_HD)
        qs_ref[:, j * LANES:(j + 1) * LANES] = (
            q * lax.rsqrt(ms + EPS) * q_gain[:, j * LANES:(j + 1) * LANES]).astype(_BF16)
    k = proj[:, OFF_KB:OFF_KB + SWA_KV]
    ms = _split_dot(k * k, seg) * (1.0 / SWA_HD)
    k = k * lax.rsqrt(ms + EPS) * kg_ref[...]
    k_sw = pltpu.roll(k, SWA_HD, axis=1)
    v = proj[:, OFF_VB:OFF_VB + SWA_KV]
    v_sw = pltpu.roll(v, SWA_HD, axis=1)
    cur = pl.ds(BLOCK, TILE)
    zero = jnp.zeros_like(k)
    kv_ref[0, cur, :] = jnp.where(low, k, zero).astype(_BF16)
    kv_ref[1, cur, :] = jnp.where(low, zero, k_sw).astype(_BF16)
    kv_ref[2, cur, :] = jnp.where(low, k_sw, zero).astype(_BF16)
    kv_ref[3, cur, :] = jnp.where(low, zero, k).astype(_BF16)
    kv_ref[4, cur, :] = v.astype(_BF16)
    kv_ref[5, cur, :] = v_sw.astype(_BF16)

    rows2 = lax.broadcasted_iota(jnp.int32, (SWA_PAIRS * BLOCK, 1), 0)
    prev_key = lax.broadcasted_iota(jnp.int32, (SWA_PAIRS * BLOCK, 2 * BLOCK), 1) < BLOCK
    low_b = lax.broadcasted_iota(jnp.int32, (BLOCK, LANES), 1) < SWA_HD
    for n in range(TILE // BLOCK):
        win = pl.ds(n * BLOCK, 2 * BLOCK)
        qrows = pl.ds(n * BLOCK, BLOCK)
        for kvh in range(SWA_KV_HEADS):
            q2 = jnp.concatenate(
                [qs_ref[qrows, (SWA_PAIRS * kvh + i) * LANES:(SWA_PAIRS * kvh + i + 1) * LANES]
                 for i in range(SWA_PAIRS)], axis=0)
            outs = []
            for par in range(2):
                s = _dot_nt(q2, kv_ref[2 * kvh + par, win, :]) + bias_ref[2 * kvh + par]
                if n == 0:
                    s = jnp.where(jnp.logical_and(first, prev_key), NEG_INF, s)
                sink = jnp.full((SWA_PAIRS * BLOCK, 1), sinks_ref[2 * SWA_PAIRS * kvh + par], _F32)
                for i in range(1, SWA_PAIRS):
                    sink = jnp.where(rows2 >= i * BLOCK,
                                     sinks_ref[2 * (SWA_PAIRS * kvh + i) + par], sink)
                m = jnp.maximum(jnp.max(s, axis=-1, keepdims=True), sink)
                p = jnp.exp(s - m)
                denom = jnp.sum(p, axis=-1, keepdims=True) + jnp.exp(sink - m)
                v_idx = 4 + ((kvh + par) % 2)
                outs.append(_dot(p.astype(_BF16), kv_ref[v_idx, win, :]) / denom)
            for i in range(SWA_PAIRS):
                blk = slice(i * BLOCK, (i + 1) * BLOCK)
                o = jnp.where(low_b, outs[0][blk], outs[1][blk])
                c0 = GLA_V + (SWA_PAIRS * kvh + i) * LANES
                mix_ref[qrows, c0:c0 + LANES] = (o * bs_ref[:, c0:c0 + LANES]).astype(_BF16)
    for a in range(6):
        kv_ref[a, 0:BLOCK, :] = kv_ref[a, TILE:TILE + BLOCK, :]


def _conformer(proj, cw_ref, cb_ref, lng_ref, lnb_ref, bs_ref, cu_ref, mix_ref):
    ca = proj[:, OFF_CA:OFF_CA + CONV_CH]
    cg = proj[:, OFF_CG:OFF_CG + CONV_CH]
    cu_ref[CONV_TAIL:CONV_TAIL + TILE, :] = ca * jax.nn.sigmoid(cg)
    rb = 64
    base = CONV_TAIL - (CONV_WIDTH - 1)
    c0 = GLA_V + SWA_Q
    for r in range(TILE // rb):
        acc = jnp.zeros((rb, CONV_CH), _F32)
        for t in range(CONV_WIDTH):
            acc = acc + cw_ref[t:t + 1, :] * cu_ref[r * rb + base + t:r * rb + base + t + rb, :]
        y = acc + cb_ref[...]
        mu = jnp.mean(y, axis=-1, keepdims=True)
        var = jnp.mean(jnp.square(y - mu), axis=-1, keepdims=True)
        y = (y - mu) * lax.rsqrt(var + EPS) * lng_ref[...] + lnb_ref[...]
        y = jax.nn.silu(y)
        mix_ref[r * rb:(r + 1) * rb, c0:c0 + CONV_CH] = (y * bs_ref[:, c0:c0 + CONV_CH]).astype(_BF16)
    cu_ref[0:CONV_TAIL, :] = cu_ref[TILE:TILE + CONV_TAIL, :]


def _mixer_kernel(sinks_ref, x_ref, g_ref, win_ref, wa2_ref, ba_ref, outg_ref, qg_ref, kg_ref,
                  bias_ref, cw_ref, cb_ref, lng_ref, lnb_ref, bs_ref, wout_ref, tri_ref,
                  seg_v_ref, seg_ref, o_ref,
                  h_ref, proj, st_ref, oa_ref, qs_ref, kv_ref, cu_ref, mix_ref):
    first = pl.program_id(1) == 0

    @pl.when(first)
    def _():
        st_ref[...] = jnp.zeros_like(st_ref)
        kv_ref[:, 0:BLOCK, :] = jnp.zeros((6, BLOCK, LANES), _BF16)
        cu_ref[0:CONV_TAIL, :] = jnp.zeros((CONV_TAIL, CONV_CH), _F32)

    h_ref[...] = _rms_rows(x_ref[0], g_ref[...]).astype(_BF16)
    for lo, hi in PROJ_GROUPS:
        proj[:, lo:hi] = _dot(h_ref[...], win_ref[:, lo:hi])

    _gla(proj, wa2_ref, ba_ref, tri_ref, st_ref, oa_ref)
    o = oa_ref[...]
    ms = _split_dot(o * o, seg_v_ref[...]) * (1.0 / GLA_DV)
    o = o * lax.rsqrt(ms + EPS)
    o = o * outg_ref[...]
    o = o * jax.nn.silu(proj[:, OFF_RA:OFF_RA + GLA_V])
    mix_ref[:, 0:GLA_V] = (o * bs_ref[:, 0:GLA_V]).astype(_BF16)

    _swa(first, proj, sinks_ref, qg_ref, kg_ref, bias_ref, seg_ref, bs_ref, qs_ref, kv_ref, mix_ref)

    _conformer(proj, cw_ref, cb_ref, lng_ref, lnb_ref, bs_ref, cu_ref, mix_ref)

    o_ref[0] = x_ref[0] + _dot(mix_ref[...], wout_ref[...])


def _const_spec(shape):
    nd = len(shape)
    return pl.BlockSpec(shape, lambda b, s: (0,) * nd, pipeline_mode=pl.Buffered(1))


def _seg_ones(n, width):
    idx = np.arange(n) // width
    return jnp.asarray((idx[:, None] == idx[None, :]).astype(np.float32), dtype=_BF16)


def _mixer(x, sinks, g, w_in_r, wa2, ba, outg, qg, kg, bias, cw, cb, lng, lnb, bs, w_out):
    B, S, _ = x.shape
    tri = jnp.asarray(np.tril(np.ones((GLA_CHUNK, GLA_CHUNK), np.float32)), dtype=_BF16)
    seg_v = _seg_ones(GLA_V, GLA_DV)
    seg = _seg_ones(LANES, SWA_HD)
    consts = (g, w_in_r, wa2, ba, outg, qg, kg, bias, cw, cb, lng, lnb, bs, w_out, tri, seg_v, seg)
    return pl.pallas_call(
        _mixer_kernel,
        grid=(B, S // TILE),
        in_specs=[pl.BlockSpec(memory_space=pltpu.SMEM),
                  pl.BlockSpec((1, TILE, D_MODEL), lambda b, s: (b, s, 0))]
        + [_const_spec(c.shape) for c in consts],
        out_specs=pl.BlockSpec((1, TILE, D_MODEL), lambda b, s: (b, s, 0)),
        scratch_shapes=[
            pltpu.VMEM((TILE, D_MODEL), _BF16),
            pltpu.VMEM((TILE, D_PROJ), _F32),
            pltpu.VMEM((GLA_V, GLA_QK), _F32),
            pltpu.VMEM((TILE, GLA_V), _F32),
            pltpu.VMEM((TILE, SWA_Q), _BF16),
            pltpu.VMEM((6, BLOCK + TILE, LANES), _BF16),
            pltpu.VMEM((CONV_TAIL + TILE, CONV_CH), _F32),
            pltpu.VMEM((TILE, D_MIX), _BF16),
        ],
        out_shape=jax.ShapeDtypeStruct(x.shape, _F32),
        compiler_params=pltpu.CompilerParams(
            dimension_semantics=("arbitrary", "arbitrary"),
            vmem_limit_bytes=VMEM_LIMIT_BYTES),
        name="mixer",
    )(sinks, x, *consts)


def _ffn_kernel(x_ref, g_ref, wup_ref, cw_ref, cb_ref, wdown_ref, o_ref,
                h_ref, u0_ref, u1_ref, tail_ref, act0_ref, act1_ref, acc_ref):
    @pl.when(pl.program_id(1) == 0)
    def _():
        tail_ref[...] = jnp.zeros_like(tail_ref)

    h_ref[...] = _rms_rows(x_ref[0], g_ref[...]).astype(_BF16)
    u_refs = (u0_ref, u1_ref)
    act_refs = (act0_ref, act1_ref)
    rb = 64
    base = FFN_TAIL - (FFN_CONV_WIDTH - 1)

    def up(c):
        u_ref = u_refs[c % 2]
        for half in range(2):
            idx = c + half * N_FFN_CHUNKS
            cols = slice(half * FFN_CHUNK, (half + 1) * FFN_CHUNK)
            u_ref[0:FFN_TAIL, cols] = tail_ref[idx]
            u_ref[FFN_TAIL:FFN_TAIL + TILE, cols] = _dot(h_ref[...], wup_ref[idx])
            tail_ref[idx] = u_ref[TILE:TILE + FFN_TAIL, cols]

    def gate(c):
        u_ref, act_ref = u_refs[c % 2], act_refs[c % 2]
        w = [jnp.concatenate([cw_ref[c, t:t + 1, :], cw_ref[c + N_FFN_CHUNKS, t:t + 1, :]], axis=-1)
             for t in range(FFN_CONV_WIDTH)]
        b = jnp.concatenate([cb_ref[c], cb_ref[c + N_FFN_CHUNKS]], axis=-1)
        for r in range(TILE // rb):
            y = b
            for t in range(FFN_CONV_WIDTH):
                y = y + w[t] * u_ref[r * rb + base + t:r * rb + base + t + rb, :]
            act = jax.nn.silu(y[:, :FFN_CHUNK]) * y[:, FFN_CHUNK:]
            act_ref[r * rb:(r + 1) * rb, :] = act.astype(_BF16)

    def down(c):
        d = _dot(act_refs[c % 2][...], wdown_ref[c])
        if c == 0:
            acc_ref[...] = d
        else:
            acc_ref[...] += d

    up(0)
    for c in range(N_FFN_CHUNKS):
        if c + 1 < N_FFN_CHUNKS:
            up(c + 1)
        gate(c)
        if c >= 1:
            down(c - 1)
    down(N_FFN_CHUNKS - 1)
    o_ref[0] = x_ref[0] + acc_ref[...]


def _ffn(x, g, w_up_r, cw_r, cb_r, w_down_r):
    B, S, _ = x.shape
    consts = (g, w_up_r, cw_r, cb_r, w_down_r)
    return pl.pallas_call(
        _ffn_kernel,
        grid=(B, S // TILE),
        in_specs=[pl.BlockSpec((1, TILE, D_MODEL), lambda b, s: (b, s, 0))]
        + [_const_spec(c.shape) for c in consts],
        out_specs=pl.BlockSpec((1, TILE, D_MODEL), lambda b, s: (b, s, 0)),
        out_shape=jax.ShapeDtypeStruct(x.shape, _F32),
        scratch_shapes=[
            pltpu.VMEM((TILE, D_MODEL), _BF16),
            pltpu.VMEM((FFN_TAIL + TILE, 2 * FFN_CHUNK), _F32),
            pltpu.VMEM((FFN_TAIL + TILE, 2 * FFN_CHUNK), _F32),
            pltpu.VMEM((2 * N_FFN_CHUNKS, FFN_TAIL, FFN_CHUNK), _F32),
            pltpu.VMEM((TILE, FFN_CHUNK), _BF16),
            pltpu.VMEM((TILE, FFN_CHUNK), _BF16),
            pltpu.VMEM((TILE, D_MODEL), _F32),
        ],
        compiler_params=pltpu.CompilerParams(
            dimension_semantics=("arbitrary", "arbitrary"),
            vmem_limit_bytes=VMEM_LIMIT_BYTES),
        name="ffn",
    )(x, *consts)


def _chunk_cols(a):
    lead = a.shape[:-1]
    a = a.reshape(lead + (2 * N_FFN_CHUNKS, FFN_CHUNK))
    return jnp.moveaxis(a, -2, 0)


def kernel(x, attn_norm_g, w_in, gla_w_a2, gla_b_a, gla_out_g, swa_q_g, swa_k_g, swa_sinks, rel_bias, conv_dw_w, conv_dw_b, conv_ln_g, conv_ln_b, branch_scale, w_out, ffn_norm_g, w_up, ffn_conv_w, ffn_conv_b, w_down):
    bias = _band_bias(rel_bias).reshape(SWA_KV_HEADS, SWA_PAIRS, 2, BLOCK, 2 * BLOCK)
    bias = bias.transpose(0, 2, 1, 3, 4).reshape(2 * SWA_KV_HEADS, SWA_PAIRS * BLOCK, 2 * BLOCK)
    z0 = 2 * GLA_QK + GLA_V
    for l in range(DEPTH):
        w = w_in[l]
        w_in_r = jnp.concatenate(
            [w[:, :2 * GLA_QK],
             jnp.pad(w[:, z0:z0 + GLA_RANK], ((0, 0), (0, LANES - GLA_RANK))),
             w[:, 2 * GLA_QK:z0],
             w[:, z0 + GLA_RANK:]], axis=1).astype(_BF16)
        wa2 = jnp.pad(gla_w_a2[l], ((0, LANES - GLA_RANK), (0, 0))).astype(_BF16)
        x = _mixer(
            x, swa_sinks[l].astype(_F32),
            attn_norm_g[l].reshape(1, D_MODEL), w_in_r, wa2,
            gla_b_a[l].reshape(1, GLA_QK), gla_out_g[l].reshape(1, GLA_V),
            jnp.tile(swa_q_g[l], SWA_HEADS).reshape(1, SWA_Q),
            jnp.tile(swa_k_g[l], SWA_KV_HEADS).reshape(1, SWA_KV),
            bias,
            jnp.pad(conv_dw_w[l], ((0, CONV_TAIL - CONV_WIDTH), (0, 0))),
            conv_dw_b[l].reshape(1, CONV_CH), conv_ln_g[l].reshape(1, CONV_CH),
            conv_ln_b[l].reshape(1, CONV_CH), branch_scale[l].reshape(1, D_MIX),
            w_out[l].astype(_BF16))
        x = _ffn(
            x, ffn_norm_g[l].reshape(1, D_MODEL),
            _chunk_cols(w_up[l].astype(_BF16)),
            _chunk_cols(jnp.pad(ffn_conv_w[l], ((0, SUBLANES - FFN_CONV_WIDTH), (0, 0)))),
            _chunk_cols(ffn_conv_b[l].reshape(1, 2 * D_FF)),
            w_down[l].astype(_BF16).reshape(N_FFN_CHUNKS, FFN_CHUNK, D_MODEL))
    return x
```

```python
import functools

import numpy as np
import jax
import jax.numpy as jnp
from jax import lax
from jax.experimental import pallas as pl
from jax.experimental.pallas import tpu as pltpu

_F32 = jnp.float32
_BF16 = jnp.bfloat16

D_MODEL = 1024
DEPTH = 2
GLA_HEADS = 4
GLA_DK = 32
GLA_DV = 64
GLA_RANK = 16
GLA_TAU = 16.0
GLA_CHUNK = 64
SWA_HEADS = 8
SWA_KV_HEADS = 2
SWA_HD = 64
WINDOW = 128
BLOCK = 128
N_BUCKETS = 32
MAX_DISTANCE = 128
CONV_CH = 256
CONV_WIDTH = 31
D_FF = 2816
FFN_CONV_WIDTH = 3
EPS = 1e-6
NEG_INF = -1e30

GLA_QK = GLA_HEADS * GLA_DK
GLA_V = GLA_HEADS * GLA_DV
SWA_Q = SWA_HEADS * SWA_HD
SWA_KV = SWA_KV_HEADS * SWA_HD
D_MIX = GLA_V + SWA_Q + CONV_CH
SWA_PAIRS = SWA_HEADS // SWA_KV_HEADS // 2

LANES = 128
SUBLANES = 8
VMEM_LIMIT_BYTES = 56 * 1024 * 1024

TILE = 512
CONV_TAIL = 32
FFN_TAIL = SUBLANES
ROW_STRIDE = 4
ROW_GROUP = ROW_STRIDE * SUBLANES
FFN_CHUNK = 256
N_FFN_CHUNKS = D_FF // FFN_CHUNK

OFF_QA = 0
OFF_KA = OFF_QA + GLA_QK
OFF_Z = OFF_KA + GLA_QK
OFF_VA = OFF_Z + LANES
OFF_RA = OFF_VA + GLA_V
OFF_QB = OFF_RA + GLA_V
OFF_KB = OFF_QB + SWA_Q
OFF_VB = OFF_KB + SWA_KV
OFF_CA = OFF_VB + SWA_KV
OFF_CG = OFF_CA + CONV_CH
D_PROJ = OFF_CG + CONV_CH
PROJ_GROUPS = ((OFF_QA, OFF_VA), (OFF_VA, OFF_QB), (OFF_QB, OFF_KB), (OFF_KB, OFF_CA), (OFF_CA, D_PROJ))


def _dot(a, b):
    return jnp.dot(a, b, preferred_element_type=_F32)


def _dot_nt(a, b):
    return lax.dot_general(a, b, (((1,), (1,)), ((), ())), preferred_element_type=_F32)


def _split_dot(x, w):
    hi = x.astype(_BF16)
    lo = (x - hi.astype(_F32)).astype(_BF16)
    return _dot(hi, w) + _dot(lo, w)


def _rms_rows(x, g):
    ms = jnp.mean(x * x, axis=-1, keepdims=True)
    return x * lax.rsqrt(ms + EPS) * g


def _bias_kernel(rb_ref, bucket_ref, out_ref):
    h = pl.program_id(0)
    bk = bucket_ref[...]
    acc = jnp.full(bk.shape, NEG_INF, _F32)
    for b in range(N_BUCKETS):
        acc = jnp.where(bk == b, rb_ref[b, h], acc)
    out_ref[0] = acc


def _band_buckets():
    i = jnp.arange(BLOCK)[:, None]
    j = jnp.arange(2 * BLOCK)[None, :]
    dist = BLOCK + i - j
    band = (dist >= 0) & (dist < WINDOW)
    max_exact = N_BUCKETS // 2
    d = jnp.maximum(dist, 0)
    d_f = jnp.maximum(d, 1).astype(_F32)
    large = max_exact + (jnp.log(d_f / max_exact) / np.float32(np.log(MAX_DISTANCE / max_exact))
                         * (N_BUCKETS - max_exact)).astype(jnp.int32)
    large = jnp.minimum(large, N_BUCKETS - 1)
    bucket = jnp.where(d < max_exact, d, large)
    return jnp.where(band, bucket, -1).astype(jnp.int32)


def _band_bias(rel_bias):
    return pl.pallas_call(
        _bias_kernel,
        grid=(SWA_HEADS,),
        in_specs=[
            pl.BlockSpec(memory_space=pltpu.SMEM),
            pl.BlockSpec((BLOCK, 2 * BLOCK), lambda h: (0, 0)),
        ],
        out_specs=pl.BlockSpec((1, BLOCK, 2 * BLOCK), lambda h: (h, 0, 0)),
        out_shape=jax.ShapeDtypeStruct((SWA_HEADS, BLOCK, 2 * BLOCK), _F32),
        name="band_bias",
    )(rel_bias.astype(_F32), _band_buckets())


def _gla(proj, wa2_ref, ba_ref, tri_ref, st_ref, oa_ref):
    row = lax.broadcasted_iota(jnp.int32, (GLA_HEADS * GLA_CHUNK, GLA_CHUNK), 0)
    col = lax.broadcasted_iota(jnp.int32, (GLA_HEADS * GLA_CHUNK, GLA_CHUNK), 1)
    causal = col <= (row % GLA_CHUNK)
    qlane = lax.broadcasted_iota(jnp.int32, (GLA_CHUNK, GLA_QK), 1) // GLA_DK
    vlane = lax.broadcasted_iota(jnp.int32, (GLA_CHUNK, GLA_V), 1) // GLA_DV
    st_row = lax.broadcasted_iota(jnp.int32, (GLA_V, GLA_QK), 0) // GLA_DV
    st_col = lax.broadcasted_iota(jnp.int32, (GLA_V, GLA_QK), 1) // GLA_DK
    same_head = st_row == st_col
    tri = tri_ref[...]
    for c in range(TILE // GLA_CHUNK):
        rows = pl.ds(c * GLA_CHUNK, GLA_CHUNK)
        q = proj[rows, OFF_QA:OFF_QA + GLA_QK] * (GLA_DK ** -0.5)
        k = proj[rows, OFF_KA:OFF_KA + GLA_QK]
        z = proj[rows, OFF_Z:OFF_Z + LANES]
        v = proj[rows, OFF_VA:OFF_VA + GLA_V]
        logit = _dot(z.astype(_BF16), wa2_ref[...]) + ba_ref[...]
        log_a = jax.nn.log_sigmoid(logit) / GLA_TAU
        bcum = _split_dot_lhs(tri, log_a)
        b_last = bcum[GLA_CHUNK - 1:GLA_CHUNK, :]
        q_dec = (q * jnp.exp(bcum)).astype(_BF16)
        k_inv = (k * jnp.exp(-bcum)).astype(_BF16)
        k_tail = (k * jnp.exp(b_last - bcum)).astype(_BF16)
        decay = jnp.exp(b_last)
        v16 = v.astype(_BF16)
        q_heads = jnp.concatenate(
            [jnp.where(qlane == h, q_dec, jnp.zeros_like(q_dec)) for h in range(GLA_HEADS)], axis=0)
        attn = jnp.where(causal, _dot_nt(q_heads, k_inv), 0.0).astype(_BF16)
        full = _dot(attn, v16)
        o = _dot_nt(q_dec, st_ref[...].astype(_BF16))
        for h in range(GLA_HEADS):
            o = o + jnp.where(vlane == h, full[h * GLA_CHUNK:(h + 1) * GLA_CHUNK, :], 0.0)
        oa_ref[rows, :] = o
        d_st = _dot(v.T.astype(_BF16), k_tail)
        st_ref[...] = st_ref[...] * decay + jnp.where(same_head, d_st, 0.0)


def _split_dot_lhs(w, x):
    hi = x.astype(_BF16)
    lo = (x - hi.astype(_F32)).astype(_BF16)
    return _dot(w, hi) + _dot(w, lo)


def _swa(first, proj, sinks_ref, qg_ref, kg_ref, bias_ref, seg_ref, bs_ref, qs_ref, kv_ref, mix_ref):
    seg = seg_ref[...]
    lane = lax.broadcasted_iota(jnp.int32, (TILE, LANES), 1)
    low = lane < SWA_HD
    q_gain = qg_ref[...] * (SWA_HD ** -0.5)
    for j in range(SWA_Q // LANES):
        q = proj[:, OFF_QB + j * LANES:OFF_QB + (j + 1) * LANES]
        ms = _split_dot(q * q, seg) * (1.0 / SWA_HD)
        qs_ref[:, j * LANES:(j + 1) * LANES] = (
            q * lax.rsqrt(ms + EPS) * q_gain[:, j * LANES:(j + 1) * LANES]).astype(_BF16)
    k = proj[:, OFF_KB:OFF_KB + SWA_KV]
    ms = _split_dot(k * k, seg) * (1.0 / SWA_HD)
    k = k * lax.rsqrt(ms + EPS) * kg_ref[...]
    k_sw = pltpu.roll(k, SWA_HD, axis=1)
    v = proj[:, OFF_VB:OFF_VB + SWA_KV]
    v_sw = pltpu.roll(v, SWA_HD, axis=1)
    cur = pl.ds(BLOCK, TILE)
    zero = jnp.zeros_like(k)
    kv_ref[0, cur, :] = jnp.where(low, k, zero).astype(_BF16)
    kv_ref[1, cur, :] = jnp.where(low, zero, k_sw).astype(_BF16)
    kv_ref[2, cur, :] = jnp.where(low, k_sw, zero).astype(_BF16)
    kv_ref[3, cur, :] = jnp.where(low, zero, k).astype(_BF16)
    kv_ref[4, cur, :] = v.astype(_BF16)
    kv_ref[5, cur, :] = v_sw.astype(_BF16)

    rows2 = lax.broadcasted_iota(jnp.int32, (SWA_PAIRS * BLOCK, 1), 0)
    prev_key = lax.broadcasted_iota(jnp.int32, (SWA_PAIRS * BLOCK, 2 * BLOCK), 1) < BLOCK
    low_b = lax.broadcasted_iota(jnp.int32, (BLOCK, LANES), 1) < SWA_HD
    for n in range(TILE // BLOCK):
        win = pl.ds(n * BLOCK, 2 * BLOCK)
        qrows = pl.ds(n * BLOCK, BLOCK)
        for kvh in range(SWA_KV_HEADS):
            q2 = jnp.concatenate(
                [qs_ref[qrows, (SWA_PAIRS * kvh + i) * LANES:(SWA_PAIRS * kvh + i + 1) * LANES]
                 for i in range(SWA_PAIRS)], axis=0)
            outs = []
            for par in range(2):
                s = _dot_nt(q2, kv_ref[2 * kvh + par, win, :]) + bias_ref[2 * kvh + par]
                if n == 0:
                    s = jnp.where(jnp.logical_and(first, prev_key), NEG_INF, s)
                sink = jnp.full((SWA_PAIRS * BLOCK, 1), sinks_ref[2 * SWA_PAIRS * kvh + par], _F32)
                for i in range(1, SWA_PAIRS):
                    sink = jnp.where(rows2 >= i * BLOCK,
                                     sinks_ref[2 * (SWA_PAIRS * kvh + i) + par], sink)
                m = jnp.maximum(jnp.max(s, axis=-1, keepdims=True), sink)
                p = jnp.exp(s - m)
                denom = jnp.sum(p, axis=-1, keepdims=True) + jnp.exp(sink - m)
                v_idx = 4 + ((kvh + par) % 2)
                outs.append(_dot(p.astype(_BF16), kv_ref[v_idx, win, :]) / denom)
            for i in range(SWA_PAIRS):
                blk = slice(i * BLOCK, (i + 1) * BLOCK)
                o = jnp.where(low_b, outs[0][blk], outs[1][blk])
                c0 = GLA_V + (SWA_PAIRS * kvh + i) * LANES
                mix_ref[qrows, c0:c0 + LANES] = (o * bs_ref[:, c0:c0 + LANES]).astype(_BF16)
    for a in range(6):
        kv_ref[a, 0:BLOCK, :] = kv_ref[a, TILE:TILE + BLOCK, :]


def _conformer(proj, cw_ref, cb_ref, lng_ref, lnb_ref, bs_ref, cu_ref, mix_ref):
    ca = proj[:, OFF_CA:OFF_CA + CONV_CH]
    cg = proj[:, OFF_CG:OFF_CG + CONV_CH]
    cu_ref[CONV_TAIL:CONV_TAIL + TILE, :] = ca * jax.nn.sigmoid(cg)
    rb = 64
    base = CONV_TAIL - (CONV_WIDTH - 1)
    c0 = GLA_V + SWA_Q
    for r in range(TILE // rb):
        acc = jnp.zeros((rb, CONV_CH), _F32)
        for t in range(CONV_WIDTH):
            acc = acc + cw_ref[t:t + 1, :] * cu_ref[r * rb + base + t:r * rb + base + t + rb, :]
        y = acc + cb_ref[...]
        mu = jnp.mean(y, axis=-1, keepdims=True)
        var = jnp.mean(jnp.square(y - mu), axis=-1, keepdims=True)
        y = (y - mu) * lax.rsqrt(var + EPS) * lng_ref[...] + lnb_ref[...]
        y = jax.nn.silu(y)
        mix_ref[r * rb:(r + 1) * rb, c0:c0 + CONV_CH] = (y * bs_ref[:, c0:c0 + CONV_CH]).astype(_BF16)
    cu_ref[0:CONV_TAIL, :] = cu_ref[TILE:TILE + CONV_TAIL, :]


def _mixer_kernel(sinks_ref, x_ref, g_ref, win_ref, wa2_ref, ba_ref, outg_ref, qg_ref, kg_ref,
                  bias_ref, cw_ref, cb_ref, lng_ref, lnb_ref, bs_ref, wout_ref, tri_ref,
                  seg_v_ref, seg_ref, o_ref,
                  h_ref, proj, st_ref, oa_ref, qs_ref, kv_ref, cu_ref, mix_ref):
    first = pl.program_id(1) == 0

    @pl.when(first)
    def _():
        st_ref[...] = jnp.zeros_like(st_ref)
        kv_ref[:, 0:BLOCK, :] = jnp.zeros((6, BLOCK, LANES), _BF16)
        cu_ref[0:CONV_TAIL, :] = jnp.zeros((CONV_TAIL, CONV_CH), _F32)

    h_ref[...] = _rms_rows(x_ref[0], g_ref[...]).astype(_BF16)
    for lo, hi in PROJ_GROUPS:
        proj[:, lo:hi] = _dot(h_ref[...], win_ref[:, lo:hi])

    _gla(proj, wa2_ref, ba_ref, tri_ref, st_ref, oa_ref)
    o = oa_ref[...]
    ms = _split_dot(o * o, seg_v_ref[...]) * (1.0 / GLA_DV)
    o = o * lax.rsqrt(ms + EPS)
    o = o * outg_ref[...]
    o = o * jax.nn.silu(proj[:, OFF_RA:OFF_RA + GLA_V])
    mix_ref[:, 0:GLA_V] = (o * bs_ref[:, 0:GLA_V]).astype(_BF16)

    _swa(first, proj, sinks_ref, qg_ref, kg_ref, bias_ref, seg_ref, bs_ref, qs_ref, kv_ref, mix_ref)

    _conformer(proj, cw_ref, cb_ref, lng_ref, lnb_ref, bs_ref, cu_ref, mix_ref)

    o_ref[0] = x_ref[0] + _dot(mix_ref[...], wout_ref[...])


def _const_spec(shape):
    nd = len(shape)
    return pl.BlockSpec(shape, lambda b, s: (0,) * nd, pipeline_mode=pl.Buffered(1))


def _seg_ones(n, width):
    idx = np.arange(n) // width
    return jnp.asarray((idx[:, None] == idx[None, :]).astype(np.float32), dtype=_BF16)


def _mixer(x, sinks, g, w_in_r, wa2, ba, outg, qg, kg, bias, cw, cb, lng, lnb, bs, w_out):
    B, S, _ = x.shape
    tri = jnp.asarray(np.tril(np.ones((GLA_CHUNK, GLA_CHUNK), np.float32)), dtype=_BF16)
    seg_v = _seg_ones(GLA_V, GLA_DV)
    seg = _seg_ones(LANES, SWA_HD)
    consts = (g, w_in_r, wa2, ba, outg, qg, kg, bias, cw, cb, lng, lnb, bs, w_out, tri, seg_v, seg)
    return pl.pallas_call(
        _mixer_kernel,
        grid=(B, S // TILE),
        in_specs=[pl.BlockSpec(memory_space=pltpu.SMEM),
                  pl.BlockSpec((1, TILE, D_MODEL), lambda b, s: (b, s, 0))]
        + [_const_spec(c.shape) for c in consts],
        out_specs=pl.BlockSpec((1, TILE, D_MODEL), lambda b, s: (b, s, 0)),
        scratch_shapes=[
            pltpu.VMEM((TILE, D_MODEL), _BF16),
            pltpu.VMEM((TILE, D_PROJ), _F32),
            pltpu.VMEM((GLA_V, GLA_QK), _F32),
            pltpu.VMEM((TILE, GLA_V), _F32),
            pltpu.VMEM((TILE, SWA_Q), _BF16),
            pltpu.VMEM((6, BLOCK + TILE, LANES), _BF16),
            pltpu.VMEM((CONV_TAIL + TILE, CONV_CH), _F32),
            pltpu.VMEM((TILE, D_MIX), _BF16),
        ],
        out_shape=jax.ShapeDtypeStruct(x.shape, _F32),
        compiler_params=pltpu.CompilerParams(
            dimension_semantics=("arbitrary", "arbitrary"),
            vmem_limit_bytes=VMEM_LIMIT_BYTES),
        name="mixer",
    )(sinks, x, *consts)


def _interleaved_rows(ref, slab, start, rows):
    parts = []
    for g in range(rows // ROW_GROUP):
        for j in range(ROW_STRIDE):
            parts.append(ref[slab, pl.ds(start + g * ROW_GROUP + j, SUBLANES, stride=ROW_STRIDE), :])
    return jnp.concatenate(parts, axis=0)


def _ffn_kernel(x_ref, g_ref, wup_ref, cw_ref, cb_ref, wdown_ref, o_ref,
                h_ref, u0_ref, u1_ref, tail_ref, act_ref, out_ref):
    @pl.when(pl.program_id(1) == 0)
    def _():
        tail_ref[...] = jnp.zeros_like(tail_ref)

    h_ref[...] = _rms_rows(x_ref[0], g_ref[...]).astype(_BF16)
    u_refs = (u0_ref, u1_ref)
    rb = 128
    slabs = FFN_CHUNK // LANES

    def up(c):
        u_ref = u_refs[c % 2]
        for half in range(2):
            idx = c + half * N_FFN_CHUNKS
            res = _dot(h_ref[...], wup_ref[idx])
            for q in range(slabs):
                s = half * slabs + q
                u_ref[s, 0:FFN_TAIL, :] = tail_ref[c, s]
                u_ref[s, FFN_TAIL:FFN_TAIL + TILE, :] = res[:, q * LANES:(q + 1) * LANES]
                tail_ref[c, s] = res[TILE - FFN_TAIL:TILE, q * LANES:(q + 1) * LANES]

    def gate(c):
        u_ref = u_refs[c % 2]
        for q in range(slabs):
            lanes = slice(q * LANES, (q + 1) * LANES)
            cols = slice(c * FFN_CHUNK + q * LANES, c * FFN_CHUNK + (q + 1) * LANES)
            for r in range(TILE // rb):
                ys = []
                for half in range(2):
                    idx = c + half * N_FFN_CHUNKS
                    y = cb_ref[idx][:, lanes]
                    for t in range(FFN_CONV_WIDTH):
                        start = FFN_TAIL + r * rb - (FFN_CONV_WIDTH - 1 - t)
                        y = y + cw_ref[idx, t:t + 1, lanes] * _interleaved_rows(
                            u_ref, half * slabs + q, start, rb)
                    ys.append(y)
                act_ref[r * rb:(r + 1) * rb, cols] = (jax.nn.silu(ys[0]) * ys[1]).astype(_BF16)

    up(0)
    for c in range(N_FFN_CHUNKS):
        if c + 1 < N_FFN_CHUNKS:
            up(c + 1)
        gate(c)
    total = _dot(act_ref[...], wdown_ref[...])

    for q in range(D_MODEL // LANES):
        lanes = slice(q * LANES, (q + 1) * LANES)
        for g in range(TILE // ROW_GROUP):
            for j in range(ROW_STRIDE):
                p = g * ROW_GROUP + j * SUBLANES
                out_ref[q, pl.ds(g * ROW_GROUP + j, SUBLANES, stride=ROW_STRIDE), :] = (
                    total[p:p + SUBLANES, lanes])
    for q in range(D_MODEL // LANES):
        lanes = slice(q * LANES, (q + 1) * LANES)
        o_ref[0, :, lanes] = x_ref[0, :, lanes] + out_ref[q]


def _ffn(x, g, w_up_r, cw_r, cb_r, w_down_r):
    B, S, _ = x.shape
    consts = (g, w_up_r, cw_r, cb_r, w_down_r)
    return pl.pallas_call(
        _ffn_kernel,
        grid=(B, S // TILE),
        in_specs=[pl.BlockSpec((1, TILE, D_MODEL), lambda b, s: (b, s, 0))]
        + [_const_spec(c.shape) for c in consts],
        out_specs=pl.BlockSpec((1, TILE, D_MODEL), lambda b, s: (b, s, 0)),
        out_shape=jax.ShapeDtypeStruct(x.shape, _F32),
        scratch_shapes=[
            pltpu.VMEM((TILE, D_MODEL), _BF16),
            pltpu.VMEM((2 * FFN_CHUNK // LANES, FFN_TAIL + TILE, LANES), _F32),
            pltpu.VMEM((2 * FFN_CHUNK // LANES, FFN_TAIL + TILE, LANES), _F32),
            pltpu.VMEM((N_FFN_CHUNKS, 2 * FFN_CHUNK // LANES, FFN_TAIL, LANES), _F32),
            pltpu.VMEM((TILE, D_FF), _BF16),
            pltpu.VMEM((D_MODEL // LANES, TILE, LANES), _F32),
        ],
        compiler_params=pltpu.CompilerParams(
            dimension_semantics=("arbitrary", "arbitrary"),
            vmem_limit_bytes=VMEM_LIMIT_BYTES),
        name="ffn",
    )(x, *consts)


def _chunk_cols(a):
    lead = a.shape[:-1]
    a = a.reshape(lead + (2 * N_FFN_CHUNKS, FFN_CHUNK))
    return jnp.moveaxis(a, -2, 0)


def kernel(x, attn_norm_g, w_in, gla_w_a2, gla_b_a, gla_out_g, swa_q_g, swa_k_g, swa_sinks, rel_bias, conv_dw_w, conv_dw_b, conv_ln_g, conv_ln_b, branch_scale, w_out, ffn_norm_g, w_up, ffn_conv_w, ffn_conv_b, w_down):
    bias = _band_bias(rel_bias).reshape(SWA_KV_HEADS, SWA_PAIRS, 2, BLOCK, 2 * BLOCK)
    bias = bias.transpose(0, 2, 1, 3, 4).reshape(2 * SWA_KV_HEADS, SWA_PAIRS * BLOCK, 2 * BLOCK)
    z0 = 2 * GLA_QK + GLA_V
    for l in range(DEPTH):
        w = w_in[l]
        w_in_r = jnp.concatenate(
            [w[:, :2 * GLA_QK],
             jnp.pad(w[:, z0:z0 + GLA_RANK], ((0, 0), (0, LANES - GLA_RANK))),
             w[:, 2 * GLA_QK:z0],
             w[:, z0 + GLA_RANK:]], axis=1).astype(_BF16)
        wa2 = jnp.pad(gla_w_a2[l], ((0, LANES - GLA_RANK), (0, 0))).astype(_BF16)
        x = _mixer(
            x, swa_sinks[l].astype(_F32),
            attn_norm_g[l].reshape(1, D_MODEL), w_in_r, wa2,
            gla_b_a[l].reshape(1, GLA_QK), gla_out_g[l].reshape(1, GLA_V),
            jnp.tile(swa_q_g[l], SWA_HEADS).reshape(1, SWA_Q),
            jnp.tile(swa_k_g[l], SWA_KV_HEADS).reshape(1, SWA_KV),
            bias,
            jnp.pad(conv_dw_w[l], ((0, CONV_TAIL - CONV_WIDTH), (0, 0))),
            conv_dw_b[l].reshape(1, CONV_CH), conv_ln_g[l].reshape(1, CONV_CH),
            conv_ln_b[l].reshape(1, CONV_CH), branch_scale[l].reshape(1, D_MIX),
            w_out[l].astype(_BF16))
        x = _ffn(
            x, ffn_norm_g[l].reshape(1, D_MODEL),
            _chunk_cols(w_up[l].astype(_BF16)),
            _chunk_cols(jnp.pad(ffn_conv_w[l], ((0, SUBLANES - FFN_CONV_WIDTH), (0, 0)))),
            _chunk_cols(ffn_conv_b[l].reshape(1, 2 * D_FF)),
            w_down[l].astype(_BF16))
    return x
```

```python
import functools

import numpy as np
import jax
import jax.numpy as jnp
from jax import lax
from jax.experimental import pallas as pl
from jax.experimental.pallas import tpu as pltpu

_F32 = jnp.float32
_BF16 = jnp.bfloat16

D_MODEL = 1024
DEPTH = 2
GLA_HEADS = 4
GLA_DK = 32
GLA_DV = 64
GLA_RANK = 16
GLA_TAU = 16.0
GLA_CHUNK = 64
SWA_HEADS = 8
SWA_KV_HEADS = 2
SWA_HD = 64
WINDOW = 128
BLOCK = 128
N_BUCKETS = 32
MAX_DISTANCE = 128
CONV_CH = 256
CONV_WIDTH = 31
D_FF = 2816
FFN_CONV_WIDTH = 3
EPS = 1e-6
NEG_INF = -1e30

GLA_QK = GLA_HEADS * GLA_DK
GLA_V = GLA_HEADS * GLA_DV
SWA_Q = SWA_HEADS * SWA_HD
SWA_KV = SWA_KV_HEADS * SWA_HD
D_MIX = GLA_V + SWA_Q + CONV_CH
SWA_PAIRS = SWA_HEADS // SWA_KV_HEADS // 2

LANES = 128
SUBLANES = 8
VMEM_LIMIT_BYTES = 56 * 1024 * 1024

TILE = 512
CONV_TAIL = 32
FFN_TAIL = SUBLANES
ROW_STRIDE = 4
ROW_GROUP = ROW_STRIDE * SUBLANES
FFN_CHUNK = 256
N_FFN_CHUNKS = D_FF // FFN_CHUNK

OFF_QA = 0
OFF_KA = OFF_QA + GLA_QK
OFF_Z = OFF_KA + GLA_QK
OFF_VA = OFF_Z + LANES
OFF_RA = OFF_VA + GLA_V
OFF_QB = OFF_RA + GLA_V
OFF_KB = OFF_QB + SWA_Q
OFF_VB = OFF_KB + SWA_KV
OFF_CA = OFF_VB + SWA_KV
OFF_CG = OFF_CA + CONV_CH
D_PROJ = OFF_CG + CONV_CH
PROJ_GROUPS = ((OFF_QA, OFF_VA), (OFF_VA, OFF_QB), (OFF_QB, OFF_KB), (OFF_KB, OFF_CA), (OFF_CA, D_PROJ))


def _dot(a, b):
    return jnp.dot(a, b, preferred_element_type=_F32)


def _dot_nt(a, b):
    return lax.dot_general(a, b, (((1,), (1,)), ((), ())), preferred_element_type=_F32)


def _split_dot(x, w):
    hi = x.astype(_BF16)
    lo = (x - hi.astype(_F32)).astype(_BF16)
    return _dot(hi, w) + _dot(lo, w)


def _rms_rows(x, g):
    ms = jnp.mean(x * x, axis=-1, keepdims=True)
    return x * lax.rsqrt(ms + EPS) * g


def _bias_kernel(rb_ref, bucket_ref, out_ref):
    h = pl.program_id(0)
    bk = bucket_ref[...]
    acc = jnp.full(bk.shape, NEG_INF, _F32)
    for b in range(N_BUCKETS):
        acc = jnp.where(bk == b, rb_ref[b, h], acc)
    out_ref[0] = acc


def _band_buckets():
    i = jnp.arange(BLOCK)[:, None]
    j = jnp.arange(2 * BLOCK)[None, :]
    dist = BLOCK + i - j
    band = (dist >= 0) & (dist < WINDOW)
    max_exact = N_BUCKETS // 2
    d = jnp.maximum(dist, 0)
    d_f = jnp.maximum(d, 1).astype(_F32)
    large = max_exact + (jnp.log(d_f / max_exact) / np.float32(np.log(MAX_DISTANCE / max_exact))
                         * (N_BUCKETS - max_exact)).astype(jnp.int32)
    large = jnp.minimum(large, N_BUCKETS - 1)
    bucket = jnp.where(d < max_exact, d, large)
    return jnp.where(band, bucket, -1).astype(jnp.int32)


def _band_bias(rel_bias):
    return pl.pallas_call(
        _bias_kernel,
        grid=(SWA_HEADS,),
        in_specs=[
            pl.BlockSpec(memory_space=pltpu.SMEM),
            pl.BlockSpec((BLOCK, 2 * BLOCK), lambda h: (0, 0)),
        ],
        out_specs=pl.BlockSpec((1, BLOCK, 2 * BLOCK), lambda h: (h, 0, 0)),
        out_shape=jax.ShapeDtypeStruct((SWA_HEADS, BLOCK, 2 * BLOCK), _F32),
        name="band_bias",
    )(rel_bias.astype(_F32), _band_buckets())


def _gla(proj, wa2_ref, ba_ref, tri_ref, st_ref, oa_ref):
    row = lax.broadcasted_iota(jnp.int32, (GLA_HEADS * GLA_CHUNK, GLA_CHUNK), 0)
    col = lax.broadcasted_iota(jnp.int32, (GLA_HEADS * GLA_CHUNK, GLA_CHUNK), 1)
    causal = col <= (row % GLA_CHUNK)
    qlane = lax.broadcasted_iota(jnp.int32, (GLA_CHUNK, GLA_QK), 1) // GLA_DK
    vlane = lax.broadcasted_iota(jnp.int32, (GLA_CHUNK, GLA_V), 1) // GLA_DV
    st_row = lax.broadcasted_iota(jnp.int32, (GLA_V, GLA_QK), 0) // GLA_DV
    st_col = lax.broadcasted_iota(jnp.int32, (GLA_V, GLA_QK), 1) // GLA_DK
    same_head = st_row == st_col
    tri = tri_ref[...]
    n_chunks = TILE // GLA_CHUNK
    chunk_rows = [slice(c * GLA_CHUNK, (c + 1) * GLA_CHUNK) for c in range(n_chunks)]

    logit = _dot(proj[:, OFF_Z:OFF_Z + LANES].astype(_BF16), wa2_ref[...]) + ba_ref[...]
    log_a = jax.nn.log_sigmoid(logit) / GLA_TAU
    la_hi = log_a.astype(_BF16)
    la_lo = (log_a - la_hi.astype(_F32)).astype(_BF16)
    bcums = [_dot(tri, la_hi[r]) + _dot(tri, la_lo[r]) for r in chunk_rows]

    q_decs, k_tails, decays, attns = [], [], [], []
    for c, r in enumerate(chunk_rows):
        bcum = bcums[c]
        q = proj[r, OFF_QA:OFF_QA + GLA_QK] * (GLA_DK ** -0.5)
        k = proj[r, OFF_KA:OFF_KA + GLA_QK]
        b_last = bcum[GLA_CHUNK - 1:GLA_CHUNK, :]
        q_dec = (q * jnp.exp(bcum)).astype(_BF16)
        k_inv = (k * jnp.exp(-bcum)).astype(_BF16)
        k_tails.append((k * jnp.exp(b_last - bcum)).astype(_BF16))
        decays.append(jnp.exp(b_last))
        q_decs.append(q_dec)
        q_heads = jnp.concatenate(
            [jnp.where(qlane == h, q_dec, jnp.zeros_like(q_dec)) for h in range(GLA_HEADS)], axis=0)
        attns.append(jnp.where(causal, _dot_nt(q_heads, k_inv), 0.0).astype(_BF16))

    o_intras, d_sts = [], []
    for c, r in enumerate(chunk_rows):
        v = proj[r, OFF_VA:OFF_VA + GLA_V]
        full = _dot(attns[c], v.astype(_BF16))
        o = jnp.where(vlane == 0, full[0:GLA_CHUNK, :], 0.0)
        for h in range(1, GLA_HEADS):
            o = o + jnp.where(vlane == h, full[h * GLA_CHUNK:(h + 1) * GLA_CHUNK, :], 0.0)
        o_intras.append(o)
        d_sts.append(_dot(v.T.astype(_BF16), k_tails[c]))

    st = st_ref[...]
    entering = []
    for c in range(n_chunks):
        entering.append(st.astype(_BF16))
        st = st * decays[c] + jnp.where(same_head, d_sts[c], 0.0)
    st_ref[...] = st

    for c, r in enumerate(chunk_rows):
        oa_ref[r, :] = _dot_nt(q_decs[c], entering[c]) + o_intras[c]


def _swa(first, proj, sinks_ref, qg_ref, kg_ref, bias_ref, seg_ref, bs_ref, qs_ref, kv_ref, mix_ref):
    seg = seg_ref[...]
    lane = lax.broadcasted_iota(jnp.int32, (TILE, LANES), 1)
    low = lane < SWA_HD
    q_gain = qg_ref[...] * (SWA_HD ** -0.5)
    for j in range(SWA_Q // LANES):
        q = proj[:, OFF_QB + j * LANES:OFF_QB + (j + 1) * LANES]
        ms = _split_dot(q * q, seg) * (1.0 / SWA_HD)
        qs_ref[:, j * LANES:(j + 1) * LANES] = (
            q * lax.rsqrt(ms + EPS) * q_gain[:, j * LANES:(j + 1) * LANES]).astype(_BF16)
    k = proj[:, OFF_KB:OFF_KB + SWA_KV]
    ms = _split_dot(k * k, seg) * (1.0 / SWA_HD)
    k = k * lax.rsqrt(ms + EPS) * kg_ref[...]
    k_sw = pltpu.roll(k, SWA_HD, axis=1)
    v = proj[:, OFF_VB:OFF_VB + SWA_KV]
    v_sw = pltpu.roll(v, SWA_HD, axis=1)
    cur = pl.ds(BLOCK, TILE)
    zero = jnp.zeros_like(k)
    kv_ref[0, cur, :] = jnp.where(low, k, zero).astype(_BF16)
    kv_ref[1, cur, :] = jnp.where(low, zero, k_sw).astype(_BF16)
    kv_ref[2, cur, :] = jnp.where(low, k_sw, zero).astype(_BF16)
    kv_ref[3, cur, :] = jnp.where(low, zero, k).astype(_BF16)
    kv_ref[4, cur, :] = v.astype(_BF16)
    kv_ref[5, cur, :] = v_sw.astype(_BF16)

    rows2 = lax.broadcasted_iota(jnp.int32, (SWA_PAIRS * BLOCK, 1), 0)
    prev_key = lax.broadcasted_iota(jnp.int32, (SWA_PAIRS * BLOCK, 2 * BLOCK), 1) < BLOCK
    low_b = lax.broadcasted_iota(jnp.int32, (BLOCK, LANES), 1) < SWA_HD
    for n in range(TILE // BLOCK):
        win = pl.ds(n * BLOCK, 2 * BLOCK)
        qrows = pl.ds(n * BLOCK, BLOCK)
        for kvh in range(SWA_KV_HEADS):
            q2 = jnp.concatenate(
                [qs_ref[qrows, (SWA_PAIRS * kvh + i) * LANES:(SWA_PAIRS * kvh + i + 1) * LANES]
                 for i in range(SWA_PAIRS)], axis=0)
            outs = []
            for par in range(2):
                s = _dot_nt(q2, kv_ref[2 * kvh + par, win, :]) + bias_ref[2 * kvh + par]
                if n == 0:
                    s = jnp.where(jnp.logical_and(first, prev_key), NEG_INF, s)
                sink = jnp.full((SWA_PAIRS * BLOCK, 1), sinks_ref[2 * SWA_PAIRS * kvh + par], _F32)
                for i in range(1, SWA_PAIRS):
                    sink = jnp.where(rows2 >= i * BLOCK,
                                     sinks_ref[2 * (SWA_PAIRS * kvh + i) + par], sink)
                m = jnp.maximum(jnp.max(s, axis=-1, keepdims=True), sink)
                p = jnp.exp(s - m)
                denom = jnp.sum(p, axis=-1, keepdims=True) + jnp.exp(sink - m)
                v_idx = 4 + ((kvh + par) % 2)
                outs.append(_dot(p.astype(_BF16), kv_ref[v_idx, win, :]) / denom)
            for i in range(SWA_PAIRS):
                blk = slice(i * BLOCK, (i + 1) * BLOCK)
                o = jnp.where(low_b, outs[0][blk], outs[1][blk])
                c0 = GLA_V + (SWA_PAIRS * kvh + i) * LANES
                mix_ref[qrows, c0:c0 + LANES] = (o * bs_ref[:, c0:c0 + LANES]).astype(_BF16)
    for a in range(6):
        kv_ref[a, 0:BLOCK, :] = kv_ref[a, TILE:TILE + BLOCK, :]


def _interleaved_rows(ref, slab, start, rows):
    parts = []
    for g in range(rows // ROW_GROUP):
        for j in range(ROW_STRIDE):
            parts.append(ref[slab, pl.ds(start + g * ROW_GROUP + j, SUBLANES, stride=ROW_STRIDE), :])
    return jnp.concatenate(parts, axis=0)


def _conformer(proj, cw_ref, cb_ref, lng_ref, lnb_ref, bs_ref, cu_ref, cy_ref, mix_ref):
    slabs = CONV_CH // LANES
    for q in range(slabs):
        lanes = slice(q * LANES, (q + 1) * LANES)
        ca = proj[:, OFF_CA + q * LANES:OFF_CA + (q + 1) * LANES]
        cg = proj[:, OFF_CG + q * LANES:OFF_CG + (q + 1) * LANES]
        cu_ref[q, CONV_TAIL:CONV_TAIL + TILE, :] = ca * jax.nn.sigmoid(cg)
    rb = 64
    c0 = GLA_V + SWA_Q
    for r in range(TILE // rb):
        ys = []
        for q in range(slabs):
            lanes = slice(q * LANES, (q + 1) * LANES)
            acc = jnp.zeros((rb, LANES), _F32)
            for t in range(CONV_WIDTH):
                start = CONV_TAIL + r * rb - (CONV_WIDTH - 1 - t)
                acc = acc + cw_ref[t:t + 1, lanes] * _interleaved_rows(cu_ref, q, start, rb)
            ys.append(acc)
        y = jnp.concatenate(ys, axis=-1) + cb_ref[...]
        mu = jnp.mean(y, axis=-1, keepdims=True)
        var = jnp.mean(jnp.square(y - mu), axis=-1, keepdims=True)
        y = (y - mu) * lax.rsqrt(var + EPS) * lng_ref[...] + lnb_ref[...]
        y = jax.nn.silu(y) * bs_ref[:, c0:c0 + CONV_CH]
        for q in range(slabs):
            for g in range(rb // ROW_GROUP):
                for j in range(ROW_STRIDE):
                    p = g * ROW_GROUP + j * SUBLANES
                    cy_ref[q, pl.ds(r * rb + g * ROW_GROUP + j, SUBLANES, stride=ROW_STRIDE), :] = (
                        y[p:p + SUBLANES, q * LANES:(q + 1) * LANES])
    for q in range(slabs):
        mix_ref[:, c0 + q * LANES:c0 + (q + 1) * LANES] = cy_ref[q].astype(_BF16)
        cu_ref[q, 0:CONV_TAIL, :] = cu_ref[q, TILE:TILE + CONV_TAIL, :]


def _mixer_kernel(sinks_ref, x_ref, g_ref, win_ref, wa2_ref, ba_ref, outg_ref, qg_ref, kg_ref,
                  bias_ref, cw_ref, cb_ref, lng_ref, lnb_ref, bs_ref, wout_ref, tri_ref,
                  seg_v_ref, seg_ref, o_ref,
                  h_ref, proj, st_ref, oa_ref, qs_ref, kv_ref, cu_ref, cy_ref, mix_ref):
    first = pl.program_id(1) == 0

    @pl.when(first)
    def _():
        st_ref[...] = jnp.zeros_like(st_ref)
        kv_ref[:, 0:BLOCK, :] = jnp.zeros((6, BLOCK, LANES), _BF16)
        cu_ref[:, 0:CONV_TAIL, :] = jnp.zeros((CONV_CH // LANES, CONV_TAIL, LANES), _F32)

    h_ref[...] = _rms_rows(x_ref[0], g_ref[...]).astype(_BF16)
    for lo, hi in PROJ_GROUPS:
        proj[:, lo:hi] = _dot(h_ref[...], win_ref[:, lo:hi])

    _gla(proj, wa2_ref, ba_ref, tri_ref, st_ref, oa_ref)
    o = oa_ref[...]
    ms = _split_dot(o * o, seg_v_ref[...]) * (1.0 / GLA_DV)
    o = o * lax.rsqrt(ms + EPS)
    o = o * outg_ref[...]
    o = o * jax.nn.silu(proj[:, OFF_RA:OFF_RA + GLA_V])
    mix_ref[:, 0:GLA_V] = (o * bs_ref[:, 0:GLA_V]).astype(_BF16)

    _swa(first, proj, sinks_ref, qg_ref, kg_ref, bias_ref, seg_ref, bs_ref, qs_ref, kv_ref, mix_ref)

    _conformer(proj, cw_ref, cb_ref, lng_ref, lnb_ref, bs_ref, cu_ref, cy_ref, mix_ref)

    o_ref[0] = x_ref[0] + _dot(mix_ref[...], wout_ref[...])


def _const_spec(shape):
    nd = len(shape)
    return pl.BlockSpec(shape, lambda b, s: (0,) * nd, pipeline_mode=pl.Buffered(1))


def _seg_ones(n, width):
    idx = np.arange(n) // width
    return jnp.asarray((idx[:, None] == idx[None, :]).astype(np.float32), dtype=_BF16)


def _mixer(x, sinks, g, w_in_r, wa2, ba, outg, qg, kg, bias, cw, cb, lng, lnb, bs, w_out):
    B, S, _ = x.shape
    tri = jnp.asarray(np.tril(np.ones((GLA_CHUNK, GLA_CHUNK), np.float32)), dtype=_BF16)
    seg_v = _seg_ones(GLA_V, GLA_DV)
    seg = _seg_ones(LANES, SWA_HD)
    consts = (g, w_in_r, wa2, ba, outg, qg, kg, bias, cw, cb, lng, lnb, bs, w_out, tri, seg_v, seg)
    return pl.pallas_call(
        _mixer_kernel,
        grid=(B, S // TILE),
        in_specs=[pl.BlockSpec(memory_space=pltpu.SMEM),
                  pl.BlockSpec((1, TILE, D_MODEL), lambda b, s: (b, s, 0))]
        + [_const_spec(c.shape) for c in consts],
        out_specs=pl.BlockSpec((1, TILE, D_MODEL), lambda b, s: (b, s, 0)),
        scratch_shapes=[
            pltpu.VMEM((TILE, D_MODEL), _BF16),
            pltpu.VMEM((TILE, D_PROJ), _F32),
            pltpu.VMEM((GLA_V, GLA_QK), _F32),
            pltpu.VMEM((TILE, GLA_V), _F32),
            pltpu.VMEM((TILE, SWA_Q), _BF16),
            pltpu.VMEM((6, BLOCK + TILE, LANES), _BF16),
            pltpu.VMEM((CONV_CH // LANES, CONV_TAIL + TILE, LANES), _F32),
            pltpu.VMEM((CONV_CH // LANES, TILE, LANES), _F32),
            pltpu.VMEM((TILE, D_MIX), _BF16),
        ],
        out_shape=jax.ShapeDtypeStruct(x.shape, _F32),
        compiler_params=pltpu.CompilerParams(
            dimension_semantics=("arbitrary", "arbitrary"),
            vmem_limit_bytes=VMEM_LIMIT_BYTES),
        name="mixer",
    )(sinks, x, *consts)


def _ffn_kernel(x_ref, g_ref, wup_ref, cw_ref, cb_ref, wdown_ref, o_ref,
                h_ref, u0_ref, u1_ref, tail_ref, act_ref, out_ref):
    @pl.when(pl.program_id(1) == 0)
    def _():
        tail_ref[...] = jnp.zeros_like(tail_ref)

    h_ref[...] = _rms_rows(x_ref[0], g_ref[...]).astype(_BF16)
    u_refs = (u0_ref, u1_ref)
    rb = 128
    slabs = FFN_CHUNK // LANES

    def up(c):
        u_ref = u_refs[c % 2]
        for half in range(2):
            idx = c + half * N_FFN_CHUNKS
            res = _dot(h_ref[...], wup_ref[idx])
            for q in range(slabs):
                s = half * slabs + q
                u_ref[s, 0:FFN_TAIL, :] = tail_ref[c, s]
                u_ref[s, FFN_TAIL:FFN_TAIL + TILE, :] = res[:, q * LANES:(q + 1) * LANES]
                tail_ref[c, s] = res[TILE - FFN_TAIL:TILE, q * LANES:(q + 1) * LANES]

    def gate(c):
        u_ref = u_refs[c % 2]
        for q in range(slabs):
            lanes = slice(q * LANES, (q + 1) * LANES)
            cols = slice(c * FFN_CHUNK + q * LANES, c * FFN_CHUNK + (q + 1) * LANES)
            for r in range(TILE // rb):
                ys = []
                for half in range(2):
                    idx = c + half * N_FFN_CHUNKS
                    y = cb_ref[idx][:, lanes]
                    for t in range(FFN_CONV_WIDTH):
                        start = FFN_TAIL + r * rb - (FFN_CONV_WIDTH - 1 - t)
                        y = y + cw_ref[idx, t:t + 1, lanes] * _interleaved_rows(
                            u_ref, half * slabs + q, start, rb)
                    ys.append(y)
                act_ref[r * rb:(r + 1) * rb, cols] = (jax.nn.silu(ys[0]) * ys[1]).astype(_BF16)

    up(0)
    for c in range(N_FFN_CHUNKS):
        if c + 1 < N_FFN_CHUNKS:
            up(c + 1)
        gate(c)
    total = _dot(act_ref[...], wdown_ref[...])

    for q in range(D_MODEL // LANES):
        lanes = slice(q * LANES, (q + 1) * LANES)
        for g in range(TILE // ROW_GROUP):
            for j in range(ROW_STRIDE):
                p = g * ROW_GROUP + j * SUBLANES
                out_ref[q, pl.ds(g * ROW_GROUP + j, SUBLANES, stride=ROW_STRIDE), :] = (
                    total[p:p + SUBLANES, lanes])
    for q in range(D_MODEL // LANES):
        lanes = slice(q * LANES, (q + 1) * LANES)
        o_ref[0, :, lanes] = x_ref[0, :, lanes] + out_ref[q]


def _ffn(x, g, w_up_r, cw_r, cb_r, w_down_r):
    B, S, _ = x.shape
    consts = (g, w_up_r, cw_r, cb_r, w_down_r)
    return pl.pallas_call(
        _ffn_kernel,
        grid=(B, S // TILE),
        in_specs=[pl.BlockSpec((1, TILE, D_MODEL), lambda b, s: (b, s, 0))]
        + [_const_spec(c.shape) for c in consts],
        out_specs=pl.BlockSpec((1, TILE, D_MODEL), lambda b, s: (b, s, 0)),
        out_shape=jax.ShapeDtypeStruct(x.shape, _F32),
        scratch_shapes=[
            pltpu.VMEM((TILE, D_MODEL), _BF16),
            pltpu.VMEM((2 * FFN_CHUNK // LANES, FFN_TAIL + TILE, LANES), _F32),
            pltpu.VMEM((2 * FFN_CHUNK // LANES, FFN_TAIL + TILE, LANES), _F32),
            pltpu.VMEM((N_FFN_CHUNKS, 2 * FFN_CHUNK // LANES, FFN_TAIL, LANES), _F32),
            pltpu.VMEM((TILE, D_FF), _BF16),
            pltpu.VMEM((D_MODEL // LANES, TILE, LANES), _F32),
        ],
        compiler_params=pltpu.CompilerParams(
            dimension_semantics=("arbitrary", "arbitrary"),
            vmem_limit_bytes=VMEM_LIMIT_BYTES),
        name="ffn",
    )(x, *consts)


def _chunk_cols(a):
    lead = a.shape[:-1]
    a = a.reshape(lead + (2 * N_FFN_CHUNKS, FFN_CHUNK))
    return jnp.moveaxis(a, -2, 0)


def kernel(x, attn_norm_g, w_in, gla_w_a2, gla_b_a, gla_out_g, swa_q_g, swa_k_g, swa_sinks, rel_bias, conv_dw_w, conv_dw_b, conv_ln_g, conv_ln_b, branch_scale, w_out, ffn_norm_g, w_up, ffn_conv_w, ffn_conv_b, w_down):
    bias = _band_bias(rel_bias).reshape(SWA_KV_HEADS, SWA_PAIRS, 2, BLOCK, 2 * BLOCK)
    bias = bias.transpose(0, 2, 1, 3, 4).reshape(2 * SWA_KV_HEADS, SWA_PAIRS * BLOCK, 2 * BLOCK)
    z0 = 2 * GLA_QK + GLA_V
    for l in range(DEPTH):
        w = w_in[l]
        w_in_r = jnp.concatenate(
            [w[:, :2 * GLA_QK],
             jnp.pad(w[:, z0:z0 + GLA_RANK], ((0, 0), (0, LANES - GLA_RANK))),
             w[:, 2 * GLA_QK:z0],
             w[:, z0 + GLA_RANK:]], axis=1).astype(_BF16)
        wa2 = jnp.pad(gla_w_a2[l], ((0, LANES - GLA_RANK), (0, 0))).astype(_BF16)
        x = _mixer(
            x, swa_sinks[l].astype(_F32),
            attn_norm_g[l].reshape(1, D_MODEL), w_in_r, wa2,
            gla_b_a[l].reshape(1, GLA_QK), gla_out_g[l].reshape(1, GLA_V),
            jnp.tile(swa_q_g[l], SWA_HEADS).reshape(1, SWA_Q),
            jnp.tile(swa_k_g[l], SWA_KV_HEADS).reshape(1, SWA_KV),
            bias,
            jnp.pad(conv_dw_w[l], ((0, CONV_TAIL - CONV_WIDTH), (0, 0))),
            conv_dw_b[l].reshape(1, CONV_CH), conv_ln_g[l].reshape(1, CONV_CH),
            conv_ln_b[l].reshape(1, CONV_CH), branch_scale[l].reshape(1, D_MIX),
            w_out[l].astype(_BF16))
        x = _ffn(
            x, ffn_norm_g[l].reshape(1, D_MODEL),
            _chunk_cols(w_up[l].astype(_BF16)),
            _chunk_cols(jnp.pad(ffn_conv_w[l], ((0, SUBLANES - FFN_CONV_WIDTH), (0, 0)))),
            _chunk_cols(ffn_conv_b[l].reshape(1, 2 * D_FF)),
            w_down[l].astype(_BF16))
    return x
```

```python
import functools

import numpy as np
import jax
import jax.numpy as jnp
from jax import lax
from jax.experimental import pallas as pl
from jax.experimental.pallas import tpu as pltpu

_F32 = jnp.float32
_BF16 = jnp.bfloat16

D_MODEL = 1024
DEPTH = 2
GLA_HEADS = 4
GLA_DK = 32
GLA_DV = 64
GLA_RANK = 16
GLA_TAU = 16.0
GLA_CHUNK = 64
SWA_HEADS = 8
SWA_KV_HEADS = 2
SWA_HD = 64
WINDOW = 128
BLOCK = 128
N_BUCKETS = 32
MAX_DISTANCE = 128
CONV_CH = 256
CONV_WIDTH = 31
D_FF = 2816
FFN_CONV_WIDTH = 3
EPS = 1e-6
NEG_INF = -1e30

GLA_QK = GLA_HEADS * GLA_DK
GLA_V = GLA_HEADS * GLA_DV
SWA_Q = SWA_HEADS * SWA_HD
SWA_KV = SWA_KV_HEADS * SWA_HD
D_MIX = GLA_V + SWA_Q + CONV_CH
SWA_PAIRS = SWA_HEADS // SWA_KV_HEADS // 2

LANES = 128
SUBLANES = 8
VMEM_LIMIT_BYTES = 56 * 1024 * 1024

TILE = 512
MIX_TILE = 2 * TILE
CONV_TAIL = 32
FFN_TAIL = SUBLANES
ROW_STRIDE = 4
ROW_GROUP = ROW_STRIDE * SUBLANES
FFN_CHUNK = 256
N_FFN_CHUNKS = D_FF // FFN_CHUNK

OFF_QA = 0
OFF_KA = OFF_QA + GLA_QK
OFF_Z = OFF_KA + GLA_QK
OFF_VA = OFF_Z + LANES
OFF_RA = OFF_VA + GLA_V
OFF_QB = OFF_RA + GLA_V
OFF_KB = OFF_QB + SWA_Q
OFF_VB = OFF_KB + SWA_KV
OFF_CA = OFF_VB + SWA_KV
OFF_CG = OFF_CA + CONV_CH
D_PROJ = OFF_CG + CONV_CH
PROJ_GROUPS = ((OFF_QA, OFF_VA), (OFF_VA, OFF_QB), (OFF_QB, OFF_KB), (OFF_KB, OFF_CA), (OFF_CA, D_PROJ))


def _dot(a, b):
    return jnp.dot(a, b, preferred_element_type=_F32)


def _dot_nt(a, b):
    return lax.dot_general(a, b, (((1,), (1,)), ((), ())), preferred_element_type=_F32)


def _split_dot(x, w):
    hi = x.astype(_BF16)
    lo = (x - hi.astype(_F32)).astype(_BF16)
    return _dot(hi, w) + _dot(lo, w)


def _rms_rows(x, g):
    ms = jnp.mean(x * x, axis=-1, keepdims=True)
    return x * lax.rsqrt(ms + EPS) * g


def _bias_kernel(rb_ref, bucket_ref, out_ref):
    h = pl.program_id(0)
    bk = bucket_ref[...]
    acc = jnp.full(bk.shape, NEG_INF, _F32)
    for b in range(N_BUCKETS):
        acc = jnp.where(bk == b, rb_ref[b, h], acc)
    out_ref[0] = acc


def _band_buckets():
    i = jnp.arange(BLOCK)[:, None]
    j = jnp.arange(2 * BLOCK)[None, :]
    dist = BLOCK + i - j
    band = (dist >= 0) & (dist < WINDOW)
    max_exact = N_BUCKETS // 2
    d = jnp.maximum(dist, 0)
    d_f = jnp.maximum(d, 1).astype(_F32)
    large = max_exact + (jnp.log(d_f / max_exact) / np.float32(np.log(MAX_DISTANCE / max_exact))
                         * (N_BUCKETS - max_exact)).astype(jnp.int32)
    large = jnp.minimum(large, N_BUCKETS - 1)
    bucket = jnp.where(d < max_exact, d, large)
    return jnp.where(band, bucket, -1).astype(jnp.int32)


def _band_bias(rel_bias):
    return pl.pallas_call(
        _bias_kernel,
        grid=(SWA_HEADS,),
        in_specs=[
            pl.BlockSpec(memory_space=pltpu.SMEM),
            pl.BlockSpec((BLOCK, 2 * BLOCK), lambda h: (0, 0)),
        ],
        out_specs=pl.BlockSpec((1, BLOCK, 2 * BLOCK), lambda h: (h, 0, 0)),
        out_shape=jax.ShapeDtypeStruct((SWA_HEADS, BLOCK, 2 * BLOCK), _F32),
        name="band_bias",
    )(rel_bias.astype(_F32), _band_buckets())


def _gla(row0, proj, wa2_ref, ba_ref, tri_ref, st_ref, oa_ref):
    row = lax.broadcasted_iota(jnp.int32, (GLA_HEADS * GLA_CHUNK, GLA_CHUNK), 0)
    col = lax.broadcasted_iota(jnp.int32, (GLA_HEADS * GLA_CHUNK, GLA_CHUNK), 1)
    causal = col <= (row % GLA_CHUNK)
    qlane = lax.broadcasted_iota(jnp.int32, (GLA_CHUNK, GLA_QK), 1) // GLA_DK
    vlane = lax.broadcasted_iota(jnp.int32, (GLA_CHUNK, GLA_V), 1) // GLA_DV
    st_row = lax.broadcasted_iota(jnp.int32, (GLA_V, GLA_QK), 0) // GLA_DV
    st_col = lax.broadcasted_iota(jnp.int32, (GLA_V, GLA_QK), 1) // GLA_DK
    same_head = st_row == st_col
    tri = tri_ref[...]
    n_chunks = TILE // GLA_CHUNK
    chunk_rows = [slice(row0 + c * GLA_CHUNK, row0 + (c + 1) * GLA_CHUNK) for c in range(n_chunks)]
    local_rows = [slice(c * GLA_CHUNK, (c + 1) * GLA_CHUNK) for c in range(n_chunks)]

    logit = _dot(proj[row0:row0 + TILE, OFF_Z:OFF_Z + LANES].astype(_BF16), wa2_ref[...]) + ba_ref[...]
    log_a = jax.nn.log_sigmoid(logit) / GLA_TAU
    la_hi = log_a.astype(_BF16)
    la_lo = (log_a - la_hi.astype(_F32)).astype(_BF16)
    bcums = [_dot(tri, la_hi[r]) + _dot(tri, la_lo[r]) for r in local_rows]

    q_decs, k_tails, decays, attns = [], [], [], []
    for c, r in enumerate(chunk_rows):
        bcum = bcums[c]
        q = proj[r, OFF_QA:OFF_QA + GLA_QK] * (GLA_DK ** -0.5)
        k = proj[r, OFF_KA:OFF_KA + GLA_QK]
        b_last = bcum[GLA_CHUNK - 1:GLA_CHUNK, :]
        q_dec = (q * jnp.exp(bcum)).astype(_BF16)
        k_inv = (k * jnp.exp(-bcum)).astype(_BF16)
        k_tails.append((k * jnp.exp(b_last - bcum)).astype(_BF16))
        decays.append(jnp.exp(b_last))
        q_decs.append(q_dec)
        q_heads = jnp.concatenate(
            [jnp.where(qlane == h, q_dec, jnp.zeros_like(q_dec)) for h in range(GLA_HEADS)], axis=0)
        attns.append(jnp.where(causal, _dot_nt(q_heads, k_inv), 0.0).astype(_BF16))

    o_intras, d_sts = [], []
    for c, r in enumerate(chunk_rows):
        v = proj[r, OFF_VA:OFF_VA + GLA_V]
        full = _dot(attns[c], v.astype(_BF16))
        o = jnp.where(vlane == 0, full[0:GLA_CHUNK, :], 0.0)
        for h in range(1, GLA_HEADS):
            o = o + jnp.where(vlane == h, full[h * GLA_CHUNK:(h + 1) * GLA_CHUNK, :], 0.0)
        o_intras.append(o)
        d_sts.append(_dot(v.T.astype(_BF16), k_tails[c]))

    st = st_ref[...]
    entering = []
    for c in range(n_chunks):
        entering.append(st.astype(_BF16))
        st = st * decays[c] + jnp.where(same_head, d_sts[c], 0.0)
    st_ref[...] = st

    for c, r in enumerate(chunk_rows):
        oa_ref[r, :] = _dot_nt(q_decs[c], entering[c]) + o_intras[c]


def _swa(row0, first, proj, sinks_ref, qg_ref, kg_ref, bias_ref, seg_ref, bs_ref, qs_ref, kv_ref, mix_ref):
    seg = seg_ref[...]
    lane = lax.broadcasted_iota(jnp.int32, (TILE, LANES), 1)
    low = lane < SWA_HD
    q_gain = qg_ref[...] * (SWA_HD ** -0.5)
    for j in range(SWA_Q // LANES):
        q = proj[row0:row0 + TILE, OFF_QB + j * LANES:OFF_QB + (j + 1) * LANES]
        ms = _split_dot(q * q, seg) * (1.0 / SWA_HD)
        qs_ref[row0:row0 + TILE, j * LANES:(j + 1) * LANES] = (
            q * lax.rsqrt(ms + EPS) * q_gain[:, j * LANES:(j + 1) * LANES]).astype(_BF16)
    k = proj[row0:row0 + TILE, OFF_KB:OFF_KB + SWA_KV]
    ms = _split_dot(k * k, seg) * (1.0 / SWA_HD)
    k = k * lax.rsqrt(ms + EPS) * kg_ref[...]
    k_sw = pltpu.roll(k, SWA_HD, axis=1)
    v = proj[row0:row0 + TILE, OFF_VB:OFF_VB + SWA_KV]
    v_sw = pltpu.roll(v, SWA_HD, axis=1)
    cur = pl.ds(BLOCK + row0, TILE)
    zero = jnp.zeros_like(k)
    kv_ref[0, cur, :] = jnp.where(low, k, zero).astype(_BF16)
    kv_ref[1, cur, :] = jnp.where(low, zero, k_sw).astype(_BF16)
    kv_ref[2, cur, :] = jnp.where(low, k_sw, zero).astype(_BF16)
    kv_ref[3, cur, :] = jnp.where(low, zero, k).astype(_BF16)
    kv_ref[4, cur, :] = v.astype(_BF16)
    kv_ref[5, cur, :] = v_sw.astype(_BF16)

    rows2 = lax.broadcasted_iota(jnp.int32, (SWA_PAIRS * BLOCK, 1), 0)
    prev_key = lax.broadcasted_iota(jnp.int32, (SWA_PAIRS * BLOCK, 2 * BLOCK), 1) < BLOCK
    low_b = lax.broadcasted_iota(jnp.int32, (BLOCK, LANES), 1) < SWA_HD
    for n in range(TILE // BLOCK):
        win = pl.ds(row0 + n * BLOCK, 2 * BLOCK)
        qrows = pl.ds(row0 + n * BLOCK, BLOCK)
        for kvh in range(SWA_KV_HEADS):
            q2 = jnp.concatenate(
                [qs_ref[qrows, (SWA_PAIRS * kvh + i) * LANES:(SWA_PAIRS * kvh + i + 1) * LANES]
                 for i in range(SWA_PAIRS)], axis=0)
            outs = []
            for par in range(2):
                s = _dot_nt(q2, kv_ref[2 * kvh + par, win, :]) + bias_ref[2 * kvh + par]
                if n == 0 and row0 == 0:
                    s = jnp.where(jnp.logical_and(first, prev_key), NEG_INF, s)
                sink = jnp.full((SWA_PAIRS * BLOCK, 1), sinks_ref[2 * SWA_PAIRS * kvh + par], _F32)
                for i in range(1, SWA_PAIRS):
                    sink = jnp.where(rows2 >= i * BLOCK,
                                     sinks_ref[2 * (SWA_PAIRS * kvh + i) + par], sink)
                m = jnp.maximum(jnp.max(s, axis=-1, keepdims=True), sink)
                p = jnp.exp(s - m)
                denom = jnp.sum(p, axis=-1, keepdims=True) + jnp.exp(sink - m)
                v_idx = 4 + ((kvh + par) % 2)
                outs.append(_dot(p.astype(_BF16), kv_ref[v_idx, win, :]) / denom)
            for i in range(SWA_PAIRS):
                blk = slice(i * BLOCK, (i + 1) * BLOCK)
                o = jnp.where(low_b, outs[0][blk], outs[1][blk])
                c0 = GLA_V + (SWA_PAIRS * kvh + i) * LANES
                mix_ref[qrows, c0:c0 + LANES] = (o * bs_ref[:, c0:c0 + LANES]).astype(_BF16)


def _interleaved_rows(ref, slab, start, rows):
    parts = []
    for g in range(rows // ROW_GROUP):
        for j in range(ROW_STRIDE):
            parts.append(ref[slab, pl.ds(start + g * ROW_GROUP + j, SUBLANES, stride=ROW_STRIDE), :])
    return jnp.concatenate(parts, axis=0)


def _conformer(row0, proj, cw_ref, cb_ref, lng_ref, lnb_ref, bs_ref, cu_ref, cy_ref, mix_ref):
    slabs = CONV_CH // LANES
    for q in range(slabs):
        ca = proj[row0:row0 + TILE, OFF_CA + q * LANES:OFF_CA + (q + 1) * LANES]
        cg = proj[row0:row0 + TILE, OFF_CG + q * LANES:OFF_CG + (q + 1) * LANES]
        cu_ref[q, CONV_TAIL + row0:CONV_TAIL + row0 + TILE, :] = ca * jax.nn.sigmoid(cg)
    rb = 64
    c0 = GLA_V + SWA_Q
    for r in range(TILE // rb):
        ys = []
        for q in range(slabs):
            lanes = slice(q * LANES, (q + 1) * LANES)
            acc = jnp.zeros((rb, LANES), _F32)
            for t in range(CONV_WIDTH):
                start = CONV_TAIL + row0 + r * rb - (CONV_WIDTH - 1 - t)
                acc = acc + cw_ref[t:t + 1, lanes] * _interleaved_rows(cu_ref, q, start, rb)
            ys.append(acc)
        y = jnp.concatenate(ys, axis=-1) + cb_ref[...]
        mu = jnp.mean(y, axis=-1, keepdims=True)
        var = jnp.mean(jnp.square(y - mu), axis=-1, keepdims=True)
        y = (y - mu) * lax.rsqrt(var + EPS) * lng_ref[...] + lnb_ref[...]
        y = jax.nn.silu(y) * bs_ref[:, c0:c0 + CONV_CH]
        for q in range(slabs):
            for g in range(rb // ROW_GROUP):
                for j in range(ROW_STRIDE):
                    p = g * ROW_GROUP + j * SUBLANES
                    cy_ref[q, pl.ds(row0 + r * rb + g * ROW_GROUP + j, SUBLANES, stride=ROW_STRIDE), :] = (
                        y[p:p + SUBLANES, q * LANES:(q + 1) * LANES])
    for q in range(slabs):
        mix_ref[row0:row0 + TILE, c0 + q * LANES:c0 + (q + 1) * LANES] = cy_ref[q, row0:row0 + TILE, :].astype(_BF16)


def _mixer_kernel(sinks_ref, x_ref, g_ref, win_ref, wa2_ref, ba_ref, outg_ref, qg_ref, kg_ref,
                  bias_ref, cw_ref, cb_ref, lng_ref, lnb_ref, bs_ref, wout_ref, tri_ref,
                  seg_v_ref, seg_ref, o_ref,
                  h_ref, proj, st_ref, oa_ref, qs_ref, kv_ref, cu_ref, cy_ref, mix_ref):
    first = pl.program_id(1) == 0

    @pl.when(first)
    def _():
        st_ref[...] = jnp.zeros_like(st_ref)
        kv_ref[:, 0:BLOCK, :] = jnp.zeros((6, BLOCK, LANES), _BF16)
        cu_ref[:, 0:CONV_TAIL, :] = jnp.zeros((CONV_CH // LANES, CONV_TAIL, LANES), _F32)

    halves = (0, TILE)

    def project(row0):
        rows = slice(row0, row0 + TILE)
        h_ref[rows, :] = _rms_rows(x_ref[0, rows, :], g_ref[...]).astype(_BF16)
        for lo, hi in PROJ_GROUPS:
            proj[rows, lo:hi] = _dot(h_ref[rows, :], win_ref[:, lo:hi])

    def branches(row0):
        rows = slice(row0, row0 + TILE)
        _gla(row0, proj, wa2_ref, ba_ref, tri_ref, st_ref, oa_ref)
        o = oa_ref[rows, :]
        ms = _split_dot(o * o, seg_v_ref[...]) * (1.0 / GLA_DV)
        o = o * lax.rsqrt(ms + EPS)
        o = o * outg_ref[...]
        o = o * jax.nn.silu(proj[rows, OFF_RA:OFF_RA + GLA_V])
        mix_ref[rows, 0:GLA_V] = (o * bs_ref[:, 0:GLA_V]).astype(_BF16)
        _swa(row0, first, proj, sinks_ref, qg_ref, kg_ref, bias_ref, seg_ref, bs_ref, qs_ref, kv_ref, mix_ref)
        _conformer(row0, proj, cw_ref, cb_ref, lng_ref, lnb_ref, bs_ref, cu_ref, cy_ref, mix_ref)

    def output(row0):
        rows = slice(row0, row0 + TILE)
        o_ref[0, rows, :] = x_ref[0, rows, :] + _dot(mix_ref[rows, :], wout_ref[...])

    project(halves[0])
    project(halves[1])
    branches(halves[0])
    output(halves[0])
    branches(halves[1])
    output(halves[1])

    for a in range(6):
        kv_ref[a, 0:BLOCK, :] = kv_ref[a, MIX_TILE:MIX_TILE + BLOCK, :]
    for q in range(CONV_CH // LANES):
        cu_ref[q, 0:CONV_TAIL, :] = cu_ref[q, MIX_TILE:MIX_TILE + CONV_TAIL, :]


def _const_spec(shape):
    nd = len(shape)
    return pl.BlockSpec(shape, lambda b, s: (0,) * nd, pipeline_mode=pl.Buffered(1))


def _seg_ones(n, width):
    idx = np.arange(n) // width
    return jnp.asarray((idx[:, None] == idx[None, :]).astype(np.float32), dtype=_BF16)


def _mixer(x, sinks, g, w_in_r, wa2, ba, outg, qg, kg, bias, cw, cb, lng, lnb, bs, w_out):
    B, S, _ = x.shape
    tri = jnp.asarray(np.tril(np.ones((GLA_CHUNK, GLA_CHUNK), np.float32)), dtype=_BF16)
    seg_v = _seg_ones(GLA_V, GLA_DV)
    seg = _seg_ones(LANES, SWA_HD)
    consts = (g, w_in_r, wa2, ba, outg, qg, kg, bias, cw, cb, lng, lnb, bs, w_out, tri, seg_v, seg)
    return pl.pallas_call(
        _mixer_kernel,
        grid=(B, S // MIX_TILE),
        in_specs=[pl.BlockSpec(memory_space=pltpu.SMEM),
                  pl.BlockSpec((1, MIX_TILE, D_MODEL), lambda b, s: (b, s, 0))]
        + [_const_spec(c.shape) for c in consts],
        out_specs=pl.BlockSpec((1, MIX_TILE, D_MODEL), lambda b, s: (b, s, 0)),
        scratch_shapes=[
            pltpu.VMEM((MIX_TILE, D_MODEL), _BF16),
            pltpu.VMEM((MIX_TILE, D_PROJ), _F32),
            pltpu.VMEM((GLA_V, GLA_QK), _F32),
            pltpu.VMEM((MIX_TILE, GLA_V), _F32),
            pltpu.VMEM((MIX_TILE, SWA_Q), _BF16),
            pltpu.VMEM((6, BLOCK + MIX_TILE, LANES), _BF16),
            pltpu.VMEM((CONV_CH // LANES, CONV_TAIL + MIX_TILE, LANES), _F32),
            pltpu.VMEM((CONV_CH // LANES, MIX_TILE, LANES), _F32),
            pltpu.VMEM((MIX_TILE, D_MIX), _BF16),
        ],
        out_shape=jax.ShapeDtypeStruct(x.shape, _F32),
        compiler_params=pltpu.CompilerParams(
            dimension_semantics=("arbitrary", "arbitrary"),
            vmem_limit_bytes=VMEM_LIMIT_BYTES),
        name="mixer",
    )(sinks, x, *consts)


def _ffn_kernel(x_ref, g_ref, wup_ref, cw_ref, cb_ref, wdown_ref, o_ref,
                h_ref, u0_ref, u1_ref, tail_ref, act_ref, out_ref):
    @pl.when(pl.program_id(1) == 0)
    def _():
        tail_ref[...] = jnp.zeros_like(tail_ref)

    h_ref[...] = _rms_rows(x_ref[0], g_ref[...]).astype(_BF16)
    u_refs = (u0_ref, u1_ref)
    rb = 128
    slabs = FFN_CHUNK // LANES

    def up(c):
        u_ref = u_refs[c % 2]
        for half in range(2):
            idx = c + half * N_FFN_CHUNKS
            res = _dot(h_ref[...], wup_ref[idx])
            for q in range(slabs):
                s = half * slabs + q
                u_ref[s, 0:FFN_TAIL, :] = tail_ref[c, s]
                u_ref[s, FFN_TAIL:FFN_TAIL + TILE, :] = res[:, q * LANES:(q + 1) * LANES]
                tail_ref[c, s] = res[TILE - FFN_TAIL:TILE, q * LANES:(q + 1) * LANES]

    def gate(c):
        u_ref = u_refs[c % 2]
        for q in range(slabs):
            lanes = slice(q * LANES, (q + 1) * LANES)
            cols = slice(c * FFN_CHUNK + q * LANES, c * FFN_CHUNK + (q + 1) * LANES)
            for r in range(TILE // rb):
                ys = []
                for half in range(2):
                    idx = c + half * N_FFN_CHUNKS
                    y = cb_ref[idx][:, lanes]
                    for t in range(FFN_CONV_WIDTH):
                        start = FFN_TAIL + r * rb - (FFN_CONV_WIDTH - 1 - t)
                        y = y + cw_ref[idx, t:t + 1, lanes] * _interleaved_rows(
                            u_ref, half * slabs + q, start, rb)
                    ys.append(y)
                act_ref[r * rb:(r + 1) * rb, cols] = (jax.nn.silu(ys[0]) * ys[1]).astype(_BF16)

    up(0)
    for c in range(N_FFN_CHUNKS):
        if c + 1 < N_FFN_CHUNKS:
            up(c + 1)
        gate(c)
    total = _dot(act_ref[...], wdown_ref[...])

    for q in range(D_MODEL // LANES):
        lanes = slice(q * LANES, (q + 1) * LANES)
        for g in range(TILE // ROW_GROUP):
            for j in range(ROW_STRIDE):
                p = g * ROW_GROUP + j * SUBLANES
                out_ref[q, pl.ds(g * ROW_GROUP + j, SUBLANES, stride=ROW_STRIDE), :] = (
                    total[p:p + SUBLANES, lanes])
    for q in range(D_MODEL // LANES):
        lanes = slice(q * LANES, (q + 1) * LANES)
        o_ref[0, :, lanes] = x_ref[0, :, lanes] + out_ref[q]


def _ffn(x, g, w_up_r, cw_r, cb_r, w_down_r):
    B, S, _ = x.shape
    consts = (g, w_up_r, cw_r, cb_r, w_down_r)
    return pl.pallas_call(
        _ffn_kernel,
        grid=(B, S // TILE),
        in_specs=[pl.BlockSpec((1, TILE, D_MODEL), lambda b, s: (b, s, 0))]
        + [_const_spec(c.shape) for c in consts],
        out_specs=pl.BlockSpec((1, TILE, D_MODEL), lambda b, s: (b, s, 0)),
        out_shape=jax.ShapeDtypeStruct(x.shape, _F32),
        scratch_shapes=[
            pltpu.VMEM((TILE, D_MODEL), _BF16),
            pltpu.VMEM((2 * FFN_CHUNK // LANES, FFN_TAIL + TILE, LANES), _F32),
            pltpu.VMEM((2 * FFN_CHUNK // LANES, FFN_TAIL + TILE, LANES), _F32),
            pltpu.VMEM((N_FFN_CHUNKS, 2 * FFN_CHUNK // LANES, FFN_TAIL, LANES), _F32),
            pltpu.VMEM((TILE, D_FF), _BF16),
            pltpu.VMEM((D_MODEL // LANES, TILE, LANES), _F32),
        ],
        compiler_params=pltpu.CompilerParams(
            dimension_semantics=("arbitrary", "arbitrary"),
            vmem_limit_bytes=VMEM_LIMIT_BYTES),
        name="ffn",
    )(x, *consts)


def _chunk_cols(a):
    lead = a.shape[:-1]
    a = a.reshape(lead + (2 * N_FFN_CHUNKS, FFN_CHUNK))
    return jnp.moveaxis(a, -2, 0)


def kernel(x, attn_norm_g, w_in, gla_w_a2, gla_b_a, gla_out_g, swa_q_g, swa_k_g, swa_sinks, rel_bias, conv_dw_w, conv_dw_b, conv_ln_g, conv_ln_b, branch_scale, w_out, ffn_norm_g, w_up, ffn_conv_w, ffn_conv_b, w_down):
    bias = _band_bias(rel_bias).reshape(SWA_KV_HEADS, SWA_PAIRS, 2, BLOCK, 2 * BLOCK)
    bias = bias.transpose(0, 2, 1, 3, 4).reshape(2 * SWA_KV_HEADS, SWA_PAIRS * BLOCK, 2 * BLOCK)
    z0 = 2 * GLA_QK + GLA_V
    for l in range(DEPTH):
        w = w_in[l]
        w_in_r = jnp.concatenate(
            [w[:, :2 * GLA_QK],
             jnp.pad(w[:, z0:z0 + GLA_RANK], ((0, 0), (0, LANES - GLA_RANK))),
             w[:, 2 * GLA_QK:z0],
             w[:, z0 + GLA_RANK:]], axis=1).astype(_BF16)
        wa2 = jnp.pad(gla_w_a2[l], ((0, LANES - GLA_RANK), (0, 0))).astype(_BF16)
        x = _mixer(
            x, swa_sinks[l].astype(_F32),
            attn_norm_g[l].reshape(1, D_MODEL), w_in_r, wa2,
            gla_b_a[l].reshape(1, GLA_QK), gla_out_g[l].reshape(1, GLA_V),
            jnp.tile(swa_q_g[l], SWA_HEADS).reshape(1, SWA_Q),
            jnp.tile(swa_k_g[l], SWA_KV_HEADS).reshape(1, SWA_KV),
            bias,
            jnp.pad(conv_dw_w[l], ((0, CONV_TAIL - CONV_WIDTH), (0, 0))),
            conv_dw_b[l].reshape(1, CONV_CH), conv_ln_g[l].reshape(1, CONV_CH),
            conv_ln_b[l].reshape(1, CONV_CH), branch_scale[l].reshape(1, D_MIX),
            w_out[l].astype(_BF16))
        x = _ffn(
            x, ffn_norm_g[l].reshape(1, D_MODEL),
            _chunk_cols(w_up[l].astype(_BF16)),
            _chunk_cols(jnp.pad(ffn_conv_w[l], ((0, SUBLANES - FFN_CONV_WIDTH), (0, 0)))),
            _chunk_cols(ffn_conv_b[l].reshape(1, 2 * D_FF)),
            w_down[l].astype(_BF16))
    return x
```

```python
import functools

import numpy as np
import jax
import jax.numpy as jnp
from jax import lax
from jax.experimental import pallas as pl
from jax.experimental.pallas import tpu as pltpu

_F32 = jnp.float32
_BF16 = jnp.bfloat16

D_MODEL = 1024
DEPTH = 2
GLA_HEADS = 4
GLA_DK = 32
GLA_DV = 64
GLA_RANK = 16
GLA_TAU = 16.0
GLA_CHUNK = 64
SWA_HEADS = 8
SWA_KV_HEADS = 2
SWA_HD = 64
WINDOW = 128
BLOCK = 128
N_BUCKETS = 32
MAX_DISTANCE = 128
CONV_CH = 256
CONV_WIDTH = 31
D_FF = 2816
FFN_CONV_WIDTH = 3
EPS = 1e-6
NEG_INF = -1e30

GLA_QK = GLA_HEADS * GLA_DK
GLA_V = GLA_HEADS * GLA_DV
SWA_Q = SWA_HEADS * SWA_HD
SWA_KV = SWA_KV_HEADS * SWA_HD
D_MIX = GLA_V + SWA_Q + CONV_CH
SWA_PAIRS = SWA_HEADS // SWA_KV_HEADS // 2

LANES = 128
SUBLANES = 8
VMEM_LIMIT_BYTES = 56 * 1024 * 1024

TILE = 512
MIX_TILE = 2 * TILE
CONV_TAIL = 32
FFN_TAIL = SUBLANES
ROW_STRIDE = 4
ROW_GROUP = ROW_STRIDE * SUBLANES
FFN_CHUNK = 256
N_FFN_CHUNKS = D_FF // FFN_CHUNK

OFF_QA = 0
OFF_KA = OFF_QA + GLA_QK
OFF_Z = OFF_KA + GLA_QK
OFF_VA = OFF_Z + LANES
OFF_RA = OFF_VA + GLA_V
OFF_QB = OFF_RA + GLA_V
OFF_KB = OFF_QB + SWA_Q
OFF_VB = OFF_KB + SWA_KV
OFF_CA = OFF_VB + SWA_KV
OFF_CG = OFF_CA + CONV_CH
D_PROJ = OFF_CG + CONV_CH
PROJ_GROUPS = ((OFF_QA, OFF_VA), (OFF_VA, OFF_QB), (OFF_QB, OFF_KB), (OFF_KB, OFF_CA), (OFF_CA, D_PROJ))


def _dot(a, b):
    return jnp.dot(a, b, preferred_element_type=_F32)


def _dot_nt(a, b):
    return lax.dot_general(a, b, (((1,), (1,)), ((), ())), preferred_element_type=_F32)


def _split_dot(x, w):
    hi = x.astype(_BF16)
    lo = (x - hi.astype(_F32)).astype(_BF16)
    return _dot(hi, w) + _dot(lo, w)


def _rms_rows(x, g):
    ms = jnp.mean(x * x, axis=-1, keepdims=True)
    return x * lax.rsqrt(ms + EPS) * g


def _bias_kernel(rb_ref, bucket_ref, out_ref):
    h = pl.program_id(0)
    bk = bucket_ref[...]
    acc = jnp.full(bk.shape, NEG_INF, _F32)
    for b in range(N_BUCKETS):
        acc = jnp.where(bk == b, rb_ref[b, h], acc)
    out_ref[0] = acc


def _band_buckets():
    i = jnp.arange(BLOCK)[:, None]
    j = jnp.arange(2 * BLOCK)[None, :]
    dist = BLOCK + i - j
    band = (dist >= 0) & (dist < WINDOW)
    max_exact = N_BUCKETS // 2
    d = jnp.maximum(dist, 0)
    d_f = jnp.maximum(d, 1).astype(_F32)
    large = max_exact + (jnp.log(d_f / max_exact) / np.float32(np.log(MAX_DISTANCE / max_exact))
                         * (N_BUCKETS - max_exact)).astype(jnp.int32)
    large = jnp.minimum(large, N_BUCKETS - 1)
    bucket = jnp.where(d < max_exact, d, large)
    return jnp.where(band, bucket, -1).astype(jnp.int32)


def _band_bias(rel_bias):
    return pl.pallas_call(
        _bias_kernel,
        grid=(SWA_HEADS,),
        in_specs=[
            pl.BlockSpec(memory_space=pltpu.SMEM),
            pl.BlockSpec((BLOCK, 2 * BLOCK), lambda h: (0, 0)),
        ],
        out_specs=pl.BlockSpec((1, BLOCK, 2 * BLOCK), lambda h: (h, 0, 0)),
        out_shape=jax.ShapeDtypeStruct((SWA_HEADS, BLOCK, 2 * BLOCK), _F32),
        name="band_bias",
    )(rel_bias.astype(_F32), _band_buckets())


def _gla(row0, proj, wa2_ref, ba_ref, tri_ref, st_ref, oa_ref):
    row = lax.broadcasted_iota(jnp.int32, (GLA_HEADS * GLA_CHUNK, GLA_CHUNK), 0)
    col = lax.broadcasted_iota(jnp.int32, (GLA_HEADS * GLA_CHUNK, GLA_CHUNK), 1)
    causal = col <= (row % GLA_CHUNK)
    qlane = lax.broadcasted_iota(jnp.int32, (GLA_CHUNK, GLA_QK), 1) // GLA_DK
    vlane = lax.broadcasted_iota(jnp.int32, (GLA_CHUNK, GLA_V), 1) // GLA_DV
    st_row = lax.broadcasted_iota(jnp.int32, (GLA_V, GLA_QK), 0) // GLA_DV
    st_col = lax.broadcasted_iota(jnp.int32, (GLA_V, GLA_QK), 1) // GLA_DK
    same_head = st_row == st_col
    tri = tri_ref[...]
    n_chunks = TILE // GLA_CHUNK
    chunk_rows = [slice(row0 + c * GLA_CHUNK, row0 + (c + 1) * GLA_CHUNK) for c in range(n_chunks)]
    local_rows = [slice(c * GLA_CHUNK, (c + 1) * GLA_CHUNK) for c in range(n_chunks)]

    logit = _dot(proj[row0:row0 + TILE, OFF_Z:OFF_Z + LANES].astype(_BF16), wa2_ref[...]) + ba_ref[...]
    log_a = jax.nn.log_sigmoid(logit) / GLA_TAU
    la_hi = log_a.astype(_BF16)
    la_lo = (log_a - la_hi.astype(_F32)).astype(_BF16)
    bcums = [_dot(tri, la_hi[r]) + _dot(tri, la_lo[r]) for r in local_rows]

    q_decs, k_tails, decays, attns = [], [], [], []
    for c, r in enumerate(chunk_rows):
        bcum = bcums[c]
        q = proj[r, OFF_QA:OFF_QA + GLA_QK] * (GLA_DK ** -0.5)
        k = proj[r, OFF_KA:OFF_KA + GLA_QK]
        b_last = bcum[GLA_CHUNK - 1:GLA_CHUNK, :]
        q_dec = (q * jnp.exp(bcum)).astype(_BF16)
        k_inv = (k * jnp.exp(-bcum)).astype(_BF16)
        k_tails.append((k * jnp.exp(b_last - bcum)).astype(_BF16))
        decays.append(jnp.exp(b_last))
        q_decs.append(q_dec)
        q_heads = jnp.concatenate(
            [jnp.where(qlane == h, q_dec, jnp.zeros_like(q_dec)) for h in range(GLA_HEADS)], axis=0)
        attns.append(jnp.where(causal, _dot_nt(q_heads, k_inv), 0.0).astype(_BF16))

    o_intras, d_sts = [], []
    for c, r in enumerate(chunk_rows):
        v = proj[r, OFF_VA:OFF_VA + GLA_V]
        full = _dot(attns[c], v.astype(_BF16))
        o = jnp.where(vlane == 0, full[0:GLA_CHUNK, :], 0.0)
        for h in range(1, GLA_HEADS):
            o = o + jnp.where(vlane == h, full[h * GLA_CHUNK:(h + 1) * GLA_CHUNK, :], 0.0)
        o_intras.append(o)
        d_sts.append(_dot(v.T.astype(_BF16), k_tails[c]))

    st = st_ref[...]
    entering = []
    for c in range(n_chunks):
        entering.append(st.astype(_BF16))
        st = st * decays[c] + jnp.where(same_head, d_sts[c], 0.0)
    st_ref[...] = st

    for c, r in enumerate(chunk_rows):
        oa_ref[r, :] = _dot_nt(q_decs[c], entering[c]) + o_intras[c]


def _swa(row0, first, proj, sinks_ref, qg_ref, kg_ref, bias_ref, seg_ref, bs_ref, qs_ref, kv_ref, mix_ref):
    seg = seg_ref[...]
    lane = lax.broadcasted_iota(jnp.int32, (TILE, LANES), 1)
    low = lane < SWA_HD
    q_gain = qg_ref[...] * (SWA_HD ** -0.5)
    for j in range(SWA_Q // LANES):
        q = proj[row0:row0 + TILE, OFF_QB + j * LANES:OFF_QB + (j + 1) * LANES]
        ms = _split_dot(q * q, seg) * (1.0 / SWA_HD)
        qs_ref[row0:row0 + TILE, j * LANES:(j + 1) * LANES] = (
            q * lax.rsqrt(ms + EPS) * q_gain[:, j * LANES:(j + 1) * LANES]).astype(_BF16)
    k = proj[row0:row0 + TILE, OFF_KB:OFF_KB + SWA_KV]
    ms = _split_dot(k * k, seg) * (1.0 / SWA_HD)
    k = k * lax.rsqrt(ms + EPS) * kg_ref[...]
    k_sw = pltpu.roll(k, SWA_HD, axis=1)
    v = proj[row0:row0 + TILE, OFF_VB:OFF_VB + SWA_KV]
    v_sw = pltpu.roll(v, SWA_HD, axis=1)
    cur = pl.ds(BLOCK + row0, TILE)
    zero = jnp.zeros_like(k)
    kv_ref[0, cur, :] = jnp.where(low, k, zero).astype(_BF16)
    kv_ref[1, cur, :] = jnp.where(low, zero, k_sw).astype(_BF16)
    kv_ref[2, cur, :] = jnp.where(low, k_sw, zero).astype(_BF16)
    kv_ref[3, cur, :] = jnp.where(low, zero, k).astype(_BF16)
    kv_ref[4, cur, :] = v.astype(_BF16)
    kv_ref[5, cur, :] = v_sw.astype(_BF16)

    rows2 = lax.broadcasted_iota(jnp.int32, (SWA_PAIRS * BLOCK, 1), 0)
    prev_key = lax.broadcasted_iota(jnp.int32, (SWA_PAIRS * BLOCK, 2 * BLOCK), 1) < BLOCK
    low_b = lax.broadcasted_iota(jnp.int32, (BLOCK, LANES), 1) < SWA_HD
    for n in range(TILE // BLOCK):
        win = pl.ds(row0 + n * BLOCK, 2 * BLOCK)
        qrows = pl.ds(row0 + n * BLOCK, BLOCK)
        for kvh in range(SWA_KV_HEADS):
            q2 = jnp.concatenate(
                [qs_ref[qrows, (SWA_PAIRS * kvh + i) * LANES:(SWA_PAIRS * kvh + i + 1) * LANES]
                 for i in range(SWA_PAIRS)], axis=0)
            outs = []
            for par in range(2):
                s = _dot_nt(q2, kv_ref[2 * kvh + par, win, :]) + bias_ref[2 * kvh + par]
                if n == 0 and row0 == 0:
                    s = jnp.where(jnp.logical_and(first, prev_key), NEG_INF, s)
                sink = jnp.full((SWA_PAIRS * BLOCK, 1), sinks_ref[2 * SWA_PAIRS * kvh + par], _F32)
                for i in range(1, SWA_PAIRS):
                    sink = jnp.where(rows2 >= i * BLOCK,
                                     sinks_ref[2 * (SWA_PAIRS * kvh + i) + par], sink)
                m = jnp.maximum(jnp.max(s, axis=-1, keepdims=True), sink)
                p = jnp.exp(s - m)
                denom = jnp.sum(p, axis=-1, keepdims=True) + jnp.exp(sink - m)
                v_idx = 4 + ((kvh + par) % 2)
                outs.append(_dot(p.astype(_BF16), kv_ref[v_idx, win, :]) / denom)
            for i in range(SWA_PAIRS):
                blk = slice(i * BLOCK, (i + 1) * BLOCK)
                o = jnp.where(low_b, outs[0][blk], outs[1][blk])
                c0 = GLA_V + (SWA_PAIRS * kvh + i) * LANES
                mix_ref[qrows, c0:c0 + LANES] = (o * bs_ref[:, c0:c0 + LANES]).astype(_BF16)


def _interleaved_rows(ref, slab, start, rows):
    parts = []
    for g in range(rows // ROW_GROUP):
        for j in range(ROW_STRIDE):
            parts.append(ref[slab, pl.ds(start + g * ROW_GROUP + j, SUBLANES, stride=ROW_STRIDE), :])
    return jnp.concatenate(parts, axis=0)


def _conformer(row0, proj, cw_ref, cb_ref, lng_ref, lnb_ref, bs_ref, cu_ref, cy_ref, mix_ref):
    slabs = CONV_CH // LANES
    for q in range(slabs):
        ca = proj[row0:row0 + TILE, OFF_CA + q * LANES:OFF_CA + (q + 1) * LANES]
        cg = proj[row0:row0 + TILE, OFF_CG + q * LANES:OFF_CG + (q + 1) * LANES]
        cu_ref[q, CONV_TAIL + row0:CONV_TAIL + row0 + TILE, :] = ca * jax.nn.sigmoid(cg)
    rb = 64
    c0 = GLA_V + SWA_Q
    for r in range(TILE // rb):
        ys = []
        for q in range(slabs):
            lanes = slice(q * LANES, (q + 1) * LANES)
            acc = jnp.zeros((rb, LANES), _F32)
            for t in range(CONV_WIDTH):
                start = CONV_TAIL + row0 + r * rb - (CONV_WIDTH - 1 - t)
                acc = acc + cw_ref[t:t + 1, lanes] * _interleaved_rows(cu_ref, q, start, rb)
            ys.append(acc)
        y = jnp.concatenate(ys, axis=-1) + cb_ref[...]
        mu = jnp.mean(y, axis=-1, keepdims=True)
        var = jnp.mean(jnp.square(y - mu), axis=-1, keepdims=True)
        y = (y - mu) * lax.rsqrt(var + EPS) * lng_ref[...] + lnb_ref[...]
        y = jax.nn.silu(y) * bs_ref[:, c0:c0 + CONV_CH]
        for q in range(slabs):
            for g in range(rb // ROW_GROUP):
                for j in range(ROW_STRIDE):
                    p = g * ROW_GROUP + j * SUBLANES
                    cy_ref[q, pl.ds(row0 + r * rb + g * ROW_GROUP + j, SUBLANES, stride=ROW_STRIDE), :] = (
                        y[p:p + SUBLANES, q * LANES:(q + 1) * LANES])
    for q in range(slabs):
        mix_ref[row0:row0 + TILE, c0 + q * LANES:c0 + (q + 1) * LANES] = cy_ref[q, row0:row0 + TILE, :].astype(_BF16)


def _mixer_kernel(sinks_ref, x_ref, g_ref, win_ref, wa2_ref, ba_ref, outg_ref, qg_ref, kg_ref,
                  bias_ref, cw_ref, cb_ref, lng_ref, lnb_ref, bs_ref, wout_ref, tri_ref,
                  seg_v_ref, seg_ref, o_ref,
                  h_ref, proj, st_ref, oa_ref, qs_ref, kv_ref, cu_ref, cy_ref, mix_ref):
    first = pl.program_id(1) == 0

    @pl.when(first)
    def _():
        st_ref[...] = jnp.zeros_like(st_ref)
        kv_ref[:, 0:BLOCK, :] = jnp.zeros((6, BLOCK, LANES), _BF16)
        cu_ref[:, 0:CONV_TAIL, :] = jnp.zeros((CONV_CH // LANES, CONV_TAIL, LANES), _F32)

    halves = (0, TILE)

    def project(row0):
        rows = slice(row0, row0 + TILE)
        h_ref[rows, :] = _rms_rows(x_ref[0, rows, :], g_ref[...]).astype(_BF16)
        for lo, hi in PROJ_GROUPS:
            proj[rows, lo:hi] = _dot(h_ref[rows, :], win_ref[:, lo:hi])

    def branches(row0):
        rows = slice(row0, row0 + TILE)
        _gla(row0, proj, wa2_ref, ba_ref, tri_ref, st_ref, oa_ref)
        o = oa_ref[rows, :]
        ms = _split_dot(o * o, seg_v_ref[...]) * (1.0 / GLA_DV)
        o = o * lax.rsqrt(ms + EPS)
        o = o * outg_ref[...]
        o = o * jax.nn.silu(proj[rows, OFF_RA:OFF_RA + GLA_V])
        mix_ref[rows, 0:GLA_V] = (o * bs_ref[:, 0:GLA_V]).astype(_BF16)
        _swa(row0, first, proj, sinks_ref, qg_ref, kg_ref, bias_ref, seg_ref, bs_ref, qs_ref, kv_ref, mix_ref)
        _conformer(row0, proj, cw_ref, cb_ref, lng_ref, lnb_ref, bs_ref, cu_ref, cy_ref, mix_ref)

    def output(row0):
        rows = slice(row0, row0 + TILE)
        o_ref[0, rows, :] = x_ref[0, rows, :] + _dot(mix_ref[rows, :], wout_ref[...])

    project(halves[0])
    project(halves[1])
    branches(halves[0])
    output(halves[0])
    branches(halves[1])
    output(halves[1])

    for a in range(6):
        kv_ref[a, 0:BLOCK, :] = kv_ref[a, MIX_TILE:MIX_TILE + BLOCK, :]
    for q in range(CONV_CH // LANES):
        cu_ref[q, 0:CONV_TAIL, :] = cu_ref[q, MIX_TILE:MIX_TILE + CONV_TAIL, :]


def _const_spec(shape):
    nd = len(shape)
    return pl.BlockSpec(shape, lambda b, s: (0,) * nd, pipeline_mode=pl.Buffered(1))


def _seg_ones(n, width):
    idx = np.arange(n) // width
    return jnp.asarray((idx[:, None] == idx[None, :]).astype(np.float32), dtype=_BF16)


def _mixer(x, sinks, g, w_in_r, wa2, ba, outg, qg, kg, bias, cw, cb, lng, lnb, bs, w_out):
    B, S, _ = x.shape
    tri = jnp.asarray(np.tril(np.ones((GLA_CHUNK, GLA_CHUNK), np.float32)), dtype=_BF16)
    seg_v = _seg_ones(GLA_V, GLA_DV)
    seg = _seg_ones(LANES, SWA_HD)
    consts = (g, w_in_r, wa2, ba, outg, qg, kg, bias, cw, cb, lng, lnb, bs, w_out, tri, seg_v, seg)
    return pl.pallas_call(
        _mixer_kernel,
        grid=(B, S // MIX_TILE),
        in_specs=[pl.BlockSpec(memory_space=pltpu.SMEM),
                  pl.BlockSpec((1, MIX_TILE, D_MODEL), lambda b, s: (b, s, 0))]
        + [_const_spec(c.shape) for c in consts],
        out_specs=pl.BlockSpec((1, MIX_TILE, D_MODEL), lambda b, s: (b, s, 0)),
        scratch_shapes=[
            pltpu.VMEM((MIX_TILE, D_MODEL), _BF16),
            pltpu.VMEM((MIX_TILE, D_PROJ), _F32),
            pltpu.VMEM((GLA_V, GLA_QK), _F32),
            pltpu.VMEM((MIX_TILE, GLA_V), _F32),
            pltpu.VMEM((MIX_TILE, SWA_Q), _BF16),
            pltpu.VMEM((6, BLOCK + MIX_TILE, LANES), _BF16),
            pltpu.VMEM((CONV_CH // LANES, CONV_TAIL + MIX_TILE, LANES), _F32),
            pltpu.VMEM((CONV_CH // LANES, MIX_TILE, LANES), _F32),
            pltpu.VMEM((MIX_TILE, D_MIX), _BF16),
        ],
        out_shape=jax.ShapeDtypeStruct(x.shape, _F32),
        compiler_params=pltpu.CompilerParams(
            dimension_semantics=("arbitrary", "arbitrary"),
            vmem_limit_bytes=VMEM_LIMIT_BYTES),
        name="mixer",
    )(sinks, x, *consts)


def _ffn_kernel(x_ref, g_ref, wup_ref, cw_ref, cb_ref, wdown_ref, o_ref,
                h_ref, u0_ref, u1_ref, tail_ref, act_ref, out_ref):
    @pl.when(pl.program_id(1) == 0)
    def _():
        tail_ref[...] = jnp.zeros_like(tail_ref)

    h_ref[...] = _rms_rows(x_ref[0], g_ref[...]).astype(_BF16)
    u_refs = (u0_ref, u1_ref)
    rb = 128
    slabs = FFN_CHUNK // LANES

    def up(c):
        u_ref = u_refs[c % 2]
        for half in range(2):
            col0 = half * D_FF + c * FFN_CHUNK
            res = _dot(h_ref[...], wup_ref[:, col0:col0 + FFN_CHUNK])
            for q in range(slabs):
                s = half * slabs + q
                u_ref[s, 0:FFN_TAIL, :] = tail_ref[c, s]
                u_ref[s, FFN_TAIL:FFN_TAIL + TILE, :] = res[:, q * LANES:(q + 1) * LANES]
                tail_ref[c, s] = res[TILE - FFN_TAIL:TILE, q * LANES:(q + 1) * LANES]

    def gate(c):
        u_ref = u_refs[c % 2]
        for q in range(slabs):
            lanes = slice(q * LANES, (q + 1) * LANES)
            cols = slice(c * FFN_CHUNK + q * LANES, c * FFN_CHUNK + (q + 1) * LANES)
            for r in range(TILE // rb):
                ys = []
                for half in range(2):
                    idx = c + half * N_FFN_CHUNKS
                    y = cb_ref[idx][:, lanes]
                    for t in range(FFN_CONV_WIDTH):
                        start = FFN_TAIL + r * rb - (FFN_CONV_WIDTH - 1 - t)
                        y = y + cw_ref[idx, t:t + 1, lanes] * _interleaved_rows(
                            u_ref, half * slabs + q, start, rb)
                    ys.append(y)
                act_ref[r * rb:(r + 1) * rb, cols] = (jax.nn.silu(ys[0]) * ys[1]).astype(_BF16)

    up(0)
    for c in range(N_FFN_CHUNKS):
        if c + 1 < N_FFN_CHUNKS:
            up(c + 1)
        gate(c)
    total = _dot(act_ref[...], wdown_ref[...])

    for q in range(D_MODEL // LANES):
        lanes = slice(q * LANES, (q + 1) * LANES)
        for g in range(TILE // ROW_GROUP):
            for j in range(ROW_STRIDE):
                p = g * ROW_GROUP + j * SUBLANES
                out_ref[q, pl.ds(g * ROW_GROUP + j, SUBLANES, stride=ROW_STRIDE), :] = (
                    total[p:p + SUBLANES, lanes])
    for q in range(D_MODEL // LANES):
        lanes = slice(q * LANES, (q + 1) * LANES)
        o_ref[0, :, lanes] = x_ref[0, :, lanes] + out_ref[q]


def _ffn(x, g, w_up_r, cw_r, cb_r, w_down_r):
    B, S, _ = x.shape
    consts = (g, w_up_r, cw_r, cb_r, w_down_r)
    return pl.pallas_call(
        _ffn_kernel,
        grid=(B, S // TILE),
        in_specs=[pl.BlockSpec((1, TILE, D_MODEL), lambda b, s: (b, s, 0))]
        + [_const_spec(c.shape) for c in consts],
        out_specs=pl.BlockSpec((1, TILE, D_MODEL), lambda b, s: (b, s, 0)),
        out_shape=jax.ShapeDtypeStruct(x.shape, _F32),
        scratch_shapes=[
            pltpu.VMEM((TILE, D_MODEL), _BF16),
            pltpu.VMEM((2 * FFN_CHUNK // LANES, FFN_TAIL + TILE, LANES), _F32),
            pltpu.VMEM((2 * FFN_CHUNK // LANES, FFN_TAIL + TILE, LANES), _F32),
            pltpu.VMEM((N_FFN_CHUNKS, 2 * FFN_CHUNK // LANES, FFN_TAIL, LANES), _F32),
            pltpu.VMEM((TILE, D_FF), _BF16),
            pltpu.VMEM((D_MODEL // LANES, TILE, LANES), _F32),
        ],
        compiler_params=pltpu.CompilerParams(
            dimension_semantics=("arbitrary", "arbitrary"),
            vmem_limit_bytes=VMEM_LIMIT_BYTES),
        name="ffn",
    )(x, *consts)


def _chunk_cols(a):
    lead = a.shape[:-1]
    a = a.reshape(lead + (2 * N_FFN_CHUNKS, FFN_CHUNK))
    return jnp.moveaxis(a, -2, 0)


def kernel(x, attn_norm_g, w_in, gla_w_a2, gla_b_a, gla_out_g, swa_q_g, swa_k_g, swa_sinks, rel_bias, conv_dw_w, conv_dw_b, conv_ln_g, conv_ln_b, branch_scale, w_out, ffn_norm_g, w_up, ffn_conv_w, ffn_conv_b, w_down):
    bias = _band_bias(rel_bias).reshape(SWA_KV_HEADS, SWA_PAIRS, 2, BLOCK, 2 * BLOCK)
    bias = bias.transpose(0, 2, 1, 3, 4).reshape(2 * SWA_KV_HEADS, SWA_PAIRS * BLOCK, 2 * BLOCK)
    z0 = 2 * GLA_QK + GLA_V
    for l in range(DEPTH):
        w = w_in[l]
        w_in_r = jnp.concatenate(
            [w[:, :2 * GLA_QK],
             jnp.pad(w[:, z0:z0 + GLA_RANK], ((0, 0), (0, LANES - GLA_RANK))),
             w[:, 2 * GLA_QK:z0],
             w[:, z0 + GLA_RANK:]], axis=1).astype(_BF16)
        wa2 = jnp.pad(gla_w_a2[l], ((0, LANES - GLA_RANK), (0, 0))).astype(_BF16)
        x = _mixer(
            x, swa_sinks[l].astype(_F32),
            attn_norm_g[l].reshape(1, D_MODEL), w_in_r, wa2,
            gla_b_a[l].reshape(1, GLA_QK), gla_out_g[l].reshape(1, GLA_V),
            jnp.tile(swa_q_g[l], SWA_HEADS).reshape(1, SWA_Q),
            jnp.tile(swa_k_g[l], SWA_KV_HEADS).reshape(1, SWA_KV),
            bias,
            jnp.pad(conv_dw_w[l], ((0, CONV_TAIL - CONV_WIDTH), (0, 0))),
            conv_dw_b[l].reshape(1, CONV_CH), conv_ln_g[l].reshape(1, CONV_CH),
            conv_ln_b[l].reshape(1, CONV_CH), branch_scale[l].reshape(1, D_MIX),
            w_out[l].astype(_BF16))
        x = _ffn(
            x, ffn_norm_g[l].reshape(1, D_MODEL),
            w_up[l].astype(_BF16),
            _chunk_cols(jnp.pad(ffn_conv_w[l], ((0, SUBLANES - FFN_CONV_WIDTH), (0, 0)))),
            _chunk_cols(ffn_conv_b[l].reshape(1, 2 * D_FF)),
            w_down[l].astype(_BF16))
    return x
```

```python
import functools

import numpy as np
import jax
import jax.numpy as jnp
from jax import lax
from jax.experimental import pallas as pl
from jax.experimental.pallas import tpu as pltpu

_F32 = jnp.float32
_BF16 = jnp.bfloat16

D_MODEL = 1024
DEPTH = 2
GLA_HEADS = 4
GLA_DK = 32
GLA_DV = 64
GLA_RANK = 16
GLA_TAU = 16.0
GLA_CHUNK = 64
SWA_HEADS = 8
SWA_KV_HEADS = 2
SWA_HD = 64
WINDOW = 128
BLOCK = 128
N_BUCKETS = 32
MAX_DISTANCE = 128
CONV_CH = 256
CONV_WIDTH = 31
D_FF = 2816
FFN_CONV_WIDTH = 3
EPS = 1e-6
NEG_INF = -1e30

GLA_QK = GLA_HEADS * GLA_DK
GLA_V = GLA_HEADS * GLA_DV
SWA_Q = SWA_HEADS * SWA_HD
SWA_KV = SWA_KV_HEADS * SWA_HD
D_MIX = GLA_V + SWA_Q + CONV_CH
SWA_PAIRS = SWA_HEADS // SWA_KV_HEADS // 2

LANES = 128
SUBLANES = 8
VMEM_LIMIT_BYTES = 56 * 1024 * 1024

TILE = 512
MIX_TILE = 2 * TILE
CONV_TAIL = 32
FFN_TAIL = SUBLANES
ROW_STRIDE = 4
ROW_GROUP = ROW_STRIDE * SUBLANES
FFN_CHUNK = 256
N_FFN_CHUNKS = D_FF // FFN_CHUNK

OFF_QA = 0
OFF_KA = OFF_QA + GLA_QK
OFF_Z = OFF_KA + GLA_QK
OFF_VA = OFF_Z + LANES
OFF_RA = OFF_VA + GLA_V
OFF_QB = OFF_RA + GLA_V
OFF_KB = OFF_QB + SWA_Q
OFF_VB = OFF_KB + SWA_KV
OFF_CA = OFF_VB + SWA_KV
OFF_CG = OFF_CA + CONV_CH
D_PROJ = OFF_CG + CONV_CH
PROJ_GROUPS = ((OFF_QA, OFF_VA), (OFF_VA, OFF_QB), (OFF_QB, OFF_KB), (OFF_KB, OFF_CA), (OFF_CA, D_PROJ))


def _dot(a, b):
    return jnp.dot(a, b, preferred_element_type=_F32)


def _dot_nt(a, b):
    return lax.dot_general(a, b, (((1,), (1,)), ((), ())), preferred_element_type=_F32)


def _split_dot(x, w):
    hi = x.astype(_BF16)
    lo = (x - hi.astype(_F32)).astype(_BF16)
    return _dot(hi, w) + _dot(lo, w)


def _rms_rows(x, g):
    ms = jnp.mean(x * x, axis=-1, keepdims=True)
    return x * lax.rsqrt(ms + EPS) * g


def _bias_kernel(rb_ref, bucket_ref, out_ref):
    h = pl.program_id(0)
    bk = bucket_ref[...]
    acc = jnp.full(bk.shape, NEG_INF, _F32)
    for b in range(N_BUCKETS):
        acc = jnp.where(bk == b, rb_ref[b, h], acc)
    out_ref[0] = acc


def _band_buckets():
    i = jnp.arange(BLOCK)[:, None]
    j = jnp.arange(2 * BLOCK)[None, :]
    dist = BLOCK + i - j
    band = (dist >= 0) & (dist < WINDOW)
    max_exact = N_BUCKETS // 2
    d = jnp.maximum(dist, 0)
    d_f = jnp.maximum(d, 1).astype(_F32)
    large = max_exact + (jnp.log(d_f / max_exact) / np.float32(np.log(MAX_DISTANCE / max_exact))
                         * (N_BUCKETS - max_exact)).astype(jnp.int32)
    large = jnp.minimum(large, N_BUCKETS - 1)
    bucket = jnp.where(d < max_exact, d, large)
    return jnp.where(band, bucket, -1).astype(jnp.int32)


def _band_bias(rel_bias):
    return pl.pallas_call(
        _bias_kernel,
        grid=(SWA_HEADS,),
        in_specs=[
            pl.BlockSpec(memory_space=pltpu.SMEM),
            pl.BlockSpec((BLOCK, 2 * BLOCK), lambda h: (0, 0)),
        ],
        out_specs=pl.BlockSpec((1, BLOCK, 2 * BLOCK), lambda h: (h, 0, 0)),
        out_shape=jax.ShapeDtypeStruct((SWA_HEADS, BLOCK, 2 * BLOCK), _F32),
        name="band_bias",
    )(rel_bias.astype(_F32), _band_buckets())


def _gla(row0, proj, wa2_ref, ba_ref, tri_ref, st_ref, oa_ref):
    row = lax.broadcasted_iota(jnp.int32, (GLA_HEADS * GLA_CHUNK, GLA_CHUNK), 0)
    col = lax.broadcasted_iota(jnp.int32, (GLA_HEADS * GLA_CHUNK, GLA_CHUNK), 1)
    causal = col <= (row % GLA_CHUNK)
    qlane = lax.broadcasted_iota(jnp.int32, (GLA_CHUNK, GLA_QK), 1) // GLA_DK
    vlane = lax.broadcasted_iota(jnp.int32, (GLA_CHUNK, GLA_V), 1) // GLA_DV
    st_row = lax.broadcasted_iota(jnp.int32, (GLA_V, GLA_QK), 0) // GLA_DV
    st_col = lax.broadcasted_iota(jnp.int32, (GLA_V, GLA_QK), 1) // GLA_DK
    same_head = st_row == st_col
    tri = tri_ref[...]
    n_chunks = TILE // GLA_CHUNK
    chunk_rows = [slice(row0 + c * GLA_CHUNK, row0 + (c + 1) * GLA_CHUNK) for c in range(n_chunks)]
    local_rows = [slice(c * GLA_CHUNK, (c + 1) * GLA_CHUNK) for c in range(n_chunks)]

    logit = _dot(proj[row0:row0 + TILE, OFF_Z:OFF_Z + LANES].astype(_BF16), wa2_ref[...]) + ba_ref[...]
    log_a = jax.nn.log_sigmoid(logit) / GLA_TAU
    la_hi = log_a.astype(_BF16)
    la_lo = (log_a - la_hi.astype(_F32)).astype(_BF16)
    bcums = [_dot(tri, la_hi[r]) + _dot(tri, la_lo[r]) for r in local_rows]

    q_decs, k_tails, decays, attns = [], [], [], []
    for c, r in enumerate(chunk_rows):
        bcum = bcums[c]
        q = proj[r, OFF_QA:OFF_QA + GLA_QK] * (GLA_DK ** -0.5)
        k = proj[r, OFF_KA:OFF_KA + GLA_QK]
        b_last = bcum[GLA_CHUNK - 1:GLA_CHUNK, :]
        q_dec = (q * jnp.exp(bcum)).astype(_BF16)
        k_inv = (k * jnp.exp(-bcum)).astype(_BF16)
        k_tails.append((k * jnp.exp(b_last - bcum)).astype(_BF16))
        decays.append(jnp.exp(b_last))
        q_decs.append(q_dec)
        q_heads = jnp.concatenate(
            [jnp.where(qlane == h, q_dec, jnp.zeros_like(q_dec)) for h in range(GLA_HEADS)], axis=0)
        attns.append(jnp.where(causal, _dot_nt(q_heads, k_inv), 0.0).astype(_BF16))

    o_intras, d_sts = [], []
    for c, r in enumerate(chunk_rows):
        v = proj[r, OFF_VA:OFF_VA + GLA_V]
        full = _dot(attns[c], v.astype(_BF16))
        o = jnp.where(vlane == 0, full[0:GLA_CHUNK, :], 0.0)
        for h in range(1, GLA_HEADS):
            o = o + jnp.where(vlane == h, full[h * GLA_CHUNK:(h + 1) * GLA_CHUNK, :], 0.0)
        o_intras.append(o)
        d_sts.append(_dot(v.T.astype(_BF16), k_tails[c]))

    st = st_ref[...]
    entering = []
    for c in range(n_chunks):
        entering.append(st.astype(_BF16))
        st = st * decays[c] + jnp.where(same_head, d_sts[c], 0.0)
    st_ref[...] = st

    for c, r in enumerate(chunk_rows):
        oa_ref[r, :] = _dot_nt(q_decs[c], entering[c]) + o_intras[c]


def _swa(row0, first, proj, sinks_ref, qg_ref, kg_ref, bias_ref, seg_ref, bs_ref, qs_ref, kv_ref, mix_ref):
    seg = seg_ref[...]
    lane = lax.broadcasted_iota(jnp.int32, (TILE, LANES), 1)
    low = lane < SWA_HD
    q_gain = qg_ref[...] * (SWA_HD ** -0.5)
    for j in range(SWA_Q // LANES):
        q = proj[row0:row0 + TILE, OFF_QB + j * LANES:OFF_QB + (j + 1) * LANES]
        ms = _split_dot(q * q, seg) * (1.0 / SWA_HD)
        qs_ref[row0:row0 + TILE, j * LANES:(j + 1) * LANES] = (
            q * lax.rsqrt(ms + EPS) * q_gain[:, j * LANES:(j + 1) * LANES]).astype(_BF16)
    k = proj[row0:row0 + TILE, OFF_KB:OFF_KB + SWA_KV]
    ms = _split_dot(k * k, seg) * (1.0 / SWA_HD)
    k = k * lax.rsqrt(ms + EPS) * kg_ref[...]
    k_sw = pltpu.roll(k, SWA_HD, axis=1)
    v = proj[row0:row0 + TILE, OFF_VB:OFF_VB + SWA_KV]
    v_sw = pltpu.roll(v, SWA_HD, axis=1)
    cur = pl.ds(BLOCK + row0, TILE)
    zero = jnp.zeros_like(k)
    kv_ref[0, cur, :] = jnp.where(low, k, zero).astype(_BF16)
    kv_ref[1, cur, :] = jnp.where(low, zero, k_sw).astype(_BF16)
    kv_ref[2, cur, :] = jnp.where(low, k_sw, zero).astype(_BF16)
    kv_ref[3, cur, :] = jnp.where(low, zero, k).astype(_BF16)
    kv_ref[4, cur, :] = v.astype(_BF16)
    kv_ref[5, cur, :] = v_sw.astype(_BF16)

    rows2 = lax.broadcasted_iota(jnp.int32, (SWA_PAIRS * BLOCK, 1), 0)
    prev_key = lax.broadcasted_iota(jnp.int32, (SWA_PAIRS * BLOCK, 2 * BLOCK), 1) < BLOCK
    low_b = lax.broadcasted_iota(jnp.int32, (BLOCK, LANES), 1) < SWA_HD
    def scores(n, kvh, par):
        win = pl.ds(row0 + n * BLOCK, 2 * BLOCK)
        qrows = pl.ds(row0 + n * BLOCK, BLOCK)
        q2 = jnp.concatenate(
            [qs_ref[qrows, (SWA_PAIRS * kvh + i) * LANES:(SWA_PAIRS * kvh + i + 1) * LANES]
             for i in range(SWA_PAIRS)], axis=0)
        s = _dot_nt(q2, kv_ref[2 * kvh + par, win, :]) + bias_ref[2 * kvh + par]
        if n == 0 and row0 == 0:
            s = jnp.where(jnp.logical_and(first, prev_key), NEG_INF, s)
        return s

    items = [(n, kvh, par) for n in range(TILE // BLOCK) for kvh in range(SWA_KV_HEADS) for par in range(2)]
    upcoming = scores(*items[0])
    outs = []
    for idx, (n, kvh, par) in enumerate(items):
        s = upcoming
        if idx + 1 < len(items):
            upcoming = scores(*items[idx + 1])
        win = pl.ds(row0 + n * BLOCK, 2 * BLOCK)
        qrows = pl.ds(row0 + n * BLOCK, BLOCK)
        sink = jnp.full((SWA_PAIRS * BLOCK, 1), sinks_ref[2 * SWA_PAIRS * kvh + par], _F32)
        for i in range(1, SWA_PAIRS):
            sink = jnp.where(rows2 >= i * BLOCK,
                             sinks_ref[2 * (SWA_PAIRS * kvh + i) + par], sink)
        m = jnp.maximum(jnp.max(s, axis=-1, keepdims=True), sink)
        p = jnp.exp(s - m)
        denom = jnp.sum(p, axis=-1, keepdims=True) + jnp.exp(sink - m)
        v_idx = 4 + ((kvh + par) % 2)
        outs.append(_dot(p.astype(_BF16), kv_ref[v_idx, win, :]) / denom)
        if par == 1:
            for i in range(SWA_PAIRS):
                blk = slice(i * BLOCK, (i + 1) * BLOCK)
                o = jnp.where(low_b, outs[0][blk], outs[1][blk])
                c0 = GLA_V + (SWA_PAIRS * kvh + i) * LANES
                mix_ref[qrows, c0:c0 + LANES] = (o * bs_ref[:, c0:c0 + LANES]).astype(_BF16)
            outs = []


def _interleaved_rows(ref, slab, start, rows):
    parts = []
    for g in range(rows // ROW_GROUP):
        for j in range(ROW_STRIDE):
            parts.append(ref[slab, pl.ds(start + g * ROW_GROUP + j, SUBLANES, stride=ROW_STRIDE), :])
    return jnp.concatenate(parts, axis=0)


def _conformer(row0, proj, cw_ref, cb_ref, lng_ref, lnb_ref, bs_ref, cu_ref, cy_ref, mix_ref):
    slabs = CONV_CH // LANES
    for q in range(slabs):
        ca = proj[row0:row0 + TILE, OFF_CA + q * LANES:OFF_CA + (q + 1) * LANES]
        cg = proj[row0:row0 + TILE, OFF_CG + q * LANES:OFF_CG + (q + 1) * LANES]
        cu_ref[q, CONV_TAIL + row0:CONV_TAIL + row0 + TILE, :] = ca * jax.nn.sigmoid(cg)
    rb = 64
    c0 = GLA_V + SWA_Q
    for r in range(TILE // rb):
        ys = []
        for q in range(slabs):
            lanes = slice(q * LANES, (q + 1) * LANES)
            acc = jnp.zeros((rb, LANES), _F32)
            for t in range(CONV_WIDTH):
                start = CONV_TAIL + row0 + r * rb - (CONV_WIDTH - 1 - t)
                acc = acc + cw_ref[t:t + 1, lanes] * _interleaved_rows(cu_ref, q, start, rb)
            ys.append(acc)
        y = jnp.concatenate(ys, axis=-1) + cb_ref[...]
        mu = jnp.mean(y, axis=-1, keepdims=True)
        var = jnp.mean(jnp.square(y - mu), axis=-1, keepdims=True)
        y = (y - mu) * lax.rsqrt(var + EPS) * lng_ref[...] + lnb_ref[...]
        y = jax.nn.silu(y) * bs_ref[:, c0:c0 + CONV_CH]
        for q in range(slabs):
            for g in range(rb // ROW_GROUP):
                for j in range(ROW_STRIDE):
                    p = g * ROW_GROUP + j * SUBLANES
                    cy_ref[q, pl.ds(row0 + r * rb + g * ROW_GROUP + j, SUBLANES, stride=ROW_STRIDE), :] = (
                        y[p:p + SUBLANES, q * LANES:(q + 1) * LANES])
    for q in range(slabs):
        mix_ref[row0:row0 + TILE, c0 + q * LANES:c0 + (q + 1) * LANES] = cy_ref[q, row0:row0 + TILE, :].astype(_BF16)


def _mixer_kernel(sinks_ref, x_ref, g_ref, win_ref, wa2_ref, ba_ref, outg_ref, qg_ref, kg_ref,
                  bias_ref, cw_ref, cb_ref, lng_ref, lnb_ref, bs_ref, wout_ref, tri_ref,
                  seg_v_ref, seg_ref, o_ref,
                  h_ref, proj, st_ref, oa_ref, qs_ref, kv_ref, cu_ref, cy_ref, mix_ref):
    first = pl.program_id(1) == 0

    @pl.when(first)
    def _():
        st_ref[...] = jnp.zeros_like(st_ref)
        kv_ref[:, 0:BLOCK, :] = jnp.zeros((6, BLOCK, LANES), _BF16)
        cu_ref[:, 0:CONV_TAIL, :] = jnp.zeros((CONV_CH // LANES, CONV_TAIL, LANES), _F32)

    halves = (0, TILE)

    def project(row0):
        rows = slice(row0, row0 + TILE)
        h_ref[rows, :] = _rms_rows(x_ref[0, rows, :], g_ref[...]).astype(_BF16)
        for lo, hi in PROJ_GROUPS:
            proj[rows, lo:hi] = _dot(h_ref[rows, :], win_ref[:, lo:hi])

    def branches(row0):
        rows = slice(row0, row0 + TILE)
        _gla(row0, proj, wa2_ref, ba_ref, tri_ref, st_ref, oa_ref)
        o = oa_ref[rows, :]
        ms = _split_dot(o * o, seg_v_ref[...]) * (1.0 / GLA_DV)
        o = o * lax.rsqrt(ms + EPS)
        o = o * outg_ref[...]
        o = o * jax.nn.silu(proj[rows, OFF_RA:OFF_RA + GLA_V])
        mix_ref[rows, 0:GLA_V] = (o * bs_ref[:, 0:GLA_V]).astype(_BF16)
        _swa(row0, first, proj, sinks_ref, qg_ref, kg_ref, bias_ref, seg_ref, bs_ref, qs_ref, kv_ref, mix_ref)
        _conformer(row0, proj, cw_ref, cb_ref, lng_ref, lnb_ref, bs_ref, cu_ref, cy_ref, mix_ref)

    def output(row0):
        rows = slice(row0, row0 + TILE)
        o_ref[0, rows, :] = x_ref[0, rows, :] + _dot(mix_ref[rows, :], wout_ref[...])

    project(halves[0])
    project(halves[1])
    branches(halves[0])
    output(halves[0])
    branches(halves[1])
    output(halves[1])

    for a in range(6):
        kv_ref[a, 0:BLOCK, :] = kv_ref[a, MIX_TILE:MIX_TILE + BLOCK, :]
    for q in range(CONV_CH // LANES):
        cu_ref[q, 0:CONV_TAIL, :] = cu_ref[q, MIX_TILE:MIX_TILE + CONV_TAIL, :]


def _const_spec(shape):
    nd = len(shape)
    return pl.BlockSpec(shape, lambda b, s: (0,) * nd, pipeline_mode=pl.Buffered(1))


def _seg_ones(n, width):
    idx = np.arange(n) // width
    return jnp.asarray((idx[:, None] == idx[None, :]).astype(np.float32), dtype=_BF16)


def _mixer(x, sinks, g, w_in_r, wa2, ba, outg, qg, kg, bias, cw, cb, lng, lnb, bs, w_out):
    B, S, _ = x.shape
    tri = jnp.asarray(np.tril(np.ones((GLA_CHUNK, GLA_CHUNK), np.float32)), dtype=_BF16)
    seg_v = _seg_ones(GLA_V, GLA_DV)
    seg = _seg_ones(LANES, SWA_HD)
    consts = (g, w_in_r, wa2, ba, outg, qg, kg, bias, cw, cb, lng, lnb, bs, w_out, tri, seg_v, seg)
    return pl.pallas_call(
        _mixer_kernel,
        grid=(B, S // MIX_TILE),
        in_specs=[pl.BlockSpec(memory_space=pltpu.SMEM),
                  pl.BlockSpec((1, MIX_TILE, D_MODEL), lambda b, s: (b, s, 0))]
        + [_const_spec(c.shape) for c in consts],
        out_specs=pl.BlockSpec((1, MIX_TILE, D_MODEL), lambda b, s: (b, s, 0)),
        scratch_shapes=[
            pltpu.VMEM((MIX_TILE, D_MODEL), _BF16),
            pltpu.VMEM((MIX_TILE, D_PROJ), _F32),
            pltpu.VMEM((GLA_V, GLA_QK), _F32),
            pltpu.VMEM((MIX_TILE, GLA_V), _F32),
            pltpu.VMEM((MIX_TILE, SWA_Q), _BF16),
            pltpu.VMEM((6, BLOCK + MIX_TILE, LANES), _BF16),
            pltpu.VMEM((CONV_CH // LANES, CONV_TAIL + MIX_TILE, LANES), _F32),
            pltpu.VMEM((CONV_CH // LANES, MIX_TILE, LANES), _F32),
            pltpu.VMEM((MIX_TILE, D_MIX), _BF16),
        ],
        out_shape=jax.ShapeDtypeStruct(x.shape, _F32),
        compiler_params=pltpu.CompilerParams(
            dimension_semantics=("arbitrary", "arbitrary"),
            vmem_limit_bytes=VMEM_LIMIT_BYTES),
        name="mixer",
    )(sinks, x, *consts)


def _ffn_kernel(x_ref, g_ref, wup_ref, cw_ref, cb_ref, wdown_ref, o_ref,
                h_ref, u0_ref, u1_ref, tail_ref, act_ref, out_ref):
    @pl.when(pl.program_id(1) == 0)
    def _():
        tail_ref[...] = jnp.zeros_like(tail_ref)

    h_ref[...] = _rms_rows(x_ref[0], g_ref[...]).astype(_BF16)
    u_refs = (u0_ref, u1_ref)
    rb = 128
    slabs = FFN_CHUNK // LANES

    def up(c):
        u_ref = u_refs[c % 2]
        for half in range(2):
            col0 = half * D_FF + c * FFN_CHUNK
            res = _dot(h_ref[...], wup_ref[:, col0:col0 + FFN_CHUNK])
            for q in range(slabs):
                s = half * slabs + q
                u_ref[s, 0:FFN_TAIL, :] = tail_ref[c, s]
                u_ref[s, FFN_TAIL:FFN_TAIL + TILE, :] = res[:, q * LANES:(q + 1) * LANES]
                tail_ref[c, s] = res[TILE - FFN_TAIL:TILE, q * LANES:(q + 1) * LANES]

    def gate(c):
        u_ref = u_refs[c % 2]
        for q in range(slabs):
            lanes = slice(q * LANES, (q + 1) * LANES)
            cols = slice(c * FFN_CHUNK + q * LANES, c * FFN_CHUNK + (q + 1) * LANES)
            for r in range(TILE // rb):
                ys = []
                for half in range(2):
                    idx = c + half * N_FFN_CHUNKS
                    y = cb_ref[idx][:, lanes]
                    for t in range(FFN_CONV_WIDTH):
                        start = FFN_TAIL + r * rb - (FFN_CONV_WIDTH - 1 - t)
                        y = y + cw_ref[idx, t:t + 1, lanes] * _interleaved_rows(
                            u_ref, half * slabs + q, start, rb)
                    ys.append(y)
                act_ref[r * rb:(r + 1) * rb, cols] = (jax.nn.silu(ys[0]) * ys[1]).astype(_BF16)

    up(0)
    for c in range(N_FFN_CHUNKS):
        if c + 1 < N_FFN_CHUNKS:
            up(c + 1)
        gate(c)
    total = _dot(act_ref[...], wdown_ref[...])

    for q in range(D_MODEL // LANES):
        lanes = slice(q * LANES, (q + 1) * LANES)
        for g in range(TILE // ROW_GROUP):
            for j in range(ROW_STRIDE):
                p = g * ROW_GROUP + j * SUBLANES
                out_ref[q, pl.ds(g * ROW_GROUP + j, SUBLANES, stride=ROW_STRIDE), :] = (
                    total[p:p + SUBLANES, lanes])
    for q in range(D_MODEL // LANES):
        lanes = slice(q * LANES, (q + 1) * LANES)
        o_ref[0, :, lanes] = x_ref[0, :, lanes] + out_ref[q]


def _ffn(x, g, w_up_r, cw_r, cb_r, w_down_r):
    B, S, _ = x.shape
    consts = (g, w_up_r, cw_r, cb_r, w_down_r)
    return pl.pallas_call(
        _ffn_kernel,
        grid=(B, S // TILE),
        in_specs=[pl.BlockSpec((1, TILE, D_MODEL), lambda b, s: (b, s, 0))]
        + [_const_spec(c.shape) for c in consts],
        out_specs=pl.BlockSpec((1, TILE, D_MODEL), lambda b, s: (b, s, 0)),
        out_shape=jax.ShapeDtypeStruct(x.shape, _F32),
        scratch_shapes=[
            pltpu.VMEM((TILE, D_MODEL), _BF16),
            pltpu.VMEM((2 * FFN_CHUNK // LANES, FFN_TAIL + TILE, LANES), _F32),
            pltpu.VMEM((2 * FFN_CHUNK // LANES, FFN_TAIL + TILE, LANES), _F32),
            pltpu.VMEM((N_FFN_CHUNKS, 2 * FFN_CHUNK // LANES, FFN_TAIL, LANES), _F32),
            pltpu.VMEM((TILE, D_FF), _BF16),
            pltpu.VMEM((D_MODEL // LANES, TILE, LANES), _F32),
        ],
        compiler_params=pltpu.CompilerParams(
            dimension_semantics=("arbitrary", "arbitrary"),
            vmem_limit_bytes=VMEM_LIMIT_BYTES),
        name="ffn",
    )(x, *consts)


def _chunk_cols(a):
    lead = a.shape[:-1]
    a = a.reshape(lead + (2 * N_FFN_CHUNKS, FFN_CHUNK))
    return jnp.moveaxis(a, -2, 0)


def kernel(x, attn_norm_g, w_in, gla_w_a2, gla_b_a, gla_out_g, swa_q_g, swa_k_g, swa_sinks, rel_bias, conv_dw_w, conv_dw_b, conv_ln_g, conv_ln_b, branch_scale, w_out, ffn_norm_g, w_up, ffn_conv_w, ffn_conv_b, w_down):
    bias = _band_bias(rel_bias).reshape(SWA_KV_HEADS, SWA_PAIRS, 2, BLOCK, 2 * BLOCK)
    bias = bias.transpose(0, 2, 1, 3, 4).reshape(2 * SWA_KV_HEADS, SWA_PAIRS * BLOCK, 2 * BLOCK)
    z0 = 2 * GLA_QK + GLA_V
    for l in range(DEPTH):
        w = w_in[l]
        w_in_r = jnp.concatenate(
            [w[:, :2 * GLA_QK],
             jnp.pad(w[:, z0:z0 + GLA_RANK], ((0, 0), (0, LANES - GLA_RANK))),
             w[:, 2 * GLA_QK:z0],
             w[:, z0 + GLA_RANK:]], axis=1).astype(_BF16)
        wa2 = jnp.pad(gla_w_a2[l], ((0, LANES - GLA_RANK), (0, 0))).astype(_BF16)
        x = _mixer(
            x, swa_sinks[l].astype(_F32),
            attn_norm_g[l].reshape(1, D_MODEL), w_in_r, wa2,
            gla_b_a[l].reshape(1, GLA_QK), gla_out_g[l].reshape(1, GLA_V),
            jnp.tile(swa_q_g[l], SWA_HEADS).reshape(1, SWA_Q),
            jnp.tile(swa_k_g[l], SWA_KV_HEADS).reshape(1, SWA_KV),
            bias,
            jnp.pad(conv_dw_w[l], ((0, CONV_TAIL - CONV_WIDTH), (0, 0))),
            conv_dw_b[l].reshape(1, CONV_CH), conv_ln_g[l].reshape(1, CONV_CH),
            conv_ln_b[l].reshape(1, CONV_CH), branch_scale[l].reshape(1, D_MIX),
            w_out[l].astype(_BF16))
        x = _ffn(
            x, ffn_norm_g[l].reshape(1, D_MODEL),
            w_up[l].astype(_BF16),
            _chunk_cols(jnp.pad(ffn_conv_w[l], ((0, SUBLANES - FFN_CONV_WIDTH), (0, 0)))),
            _chunk_cols(ffn_conv_b[l].reshape(1, 2 * D_FF)),
            w_down[l].astype(_BF16))
    return x
```
